```python
import jax, jax.numpy as jnp
from jax import lax
import numpy as np

D_MODEL = 1024
BATCH = 2
SEQ = 16384
DEPTH = 2

CHUNK = 64
N_MIXERS = 4
GROUP_WIDTH = D_MODEL // N_MIXERS
HEAD_DIM = 64
N_HEADS = GROUP_WIDTH // HEAD_DIM
RWKV_W_LORA = 64
RWKV_A_LORA = 64
RWKV_GN_EPS = 64e-5
POOL_WINDOWS = (2, 4, 8, 16)
POOL_GROUP = GROUP_WIDTH // len(POOL_WINDOWS)
MLSTM_CONV = 4
ROPE_BASE = 10000.0
D_PLE = 256
LN_EPS = 1e-5
HEAD_NORM_EPS = 1e-6
DEEPNORM_ALPHA = (2.0 * DEPTH) ** 0.25
DEEPNORM_BETA = (8.0 * DEPTH) ** -0.25
RWKV_SHIFT_COLS = 3 * GROUP_WIDTH + RWKV_W_LORA + RWKV_A_LORA
COL_SIZES = (RWKV_SHIFT_COLS, GROUP_WIDTH,
             GROUP_WIDTH, GROUP_WIDTH,
             2 * GROUP_WIDTH, GROUP_WIDTH, N_HEADS, N_HEADS,
             GROUP_WIDTH, GROUP_WIDTH,
             GROUP_WIDTH, GROUP_WIDTH, GROUP_WIDTH, GROUP_WIDTH)
N_COLS = sum(COL_SIZES)

kernel_name = "hymba_style_rwkv7_pool_mlstm_retnet_deepnorm"


def _split_columns(z):
    idx = np.cumsum(COL_SIZES)[:-1].tolist()
    return jnp.split(z, idx, axis=-1)


def _layer_norm(x, g, b, eps):
    xf = x.astype(jnp.float32)
    mu = jnp.mean(xf, -1, keepdims=True)
    var = jnp.mean(jnp.square(xf - mu), -1, keepdims=True)
    return (xf - mu) * lax.rsqrt(var + eps) * g + b


def _head_norm(y, eps):
    y = y.astype(jnp.float32)
    mu = jnp.mean(y, -1, keepdims=True)
    var = jnp.mean(jnp.square(y - mu), -1, keepdims=True)
    return (y - mu) * lax.rsqrt(var + eps)


def _rotary(z, cos, sin):
    z1, z2 = z[..., ::2], z[..., 1::2]
    c, s = cos[:, None, :], sin[:, None, :]
    return jnp.stack([z1 * c - z2 * s, z1 * s + z2 * c], axis=-1).reshape(z.shape)


def _causal_dwconv(z, w):
    kw = w.shape[0]
    zp = jnp.pad(z, ((0, 0), (kw - 1, 0), (0, 0)))
    return lax.conv_general_dilated(zp, w[:, None, :].astype(z.dtype), (1,), 'VALID',
                                    dimension_numbers=('NWC', 'WIO', 'NWC'),
                                    feature_group_count=z.shape[-1])


def rwkv7_mix(z, mu, w0, w2, a0, a2, k_k, k_a, r_k, ln_w, ln_b):
    f32 = jnp.float32
    B, S, _ = z.shape
    G = GROUP_WIDTH
    prev = jnp.pad(z, ((0, 0), (1, 0), (0, 0)))[:, :-1]
    z = z + (prev - z) * mu
    r, k, v, wl, al = jnp.split(z, [G, 2 * G, 3 * G, 3 * G + RWKV_W_LORA], axis=-1)
    w_log = -jax.nn.softplus(-(w0 + jnp.tanh(wl) @ w2).astype(f32)) - 0.5
    decay = jnp.exp(-jnp.exp(w_log))
    a = jax.nn.sigmoid((a0 + al @ a2).astype(f32))
    heads = lambda t: t.astype(f32).reshape(B, S, N_HEADS, HEAD_DIM)
    kk = heads(k * k_k)
    kk = kk / jnp.maximum(jnp.sqrt(jnp.sum(kk * kk, -1, keepdims=True)), 1e-12)
    r, v, decay, a = heads(r), heads(v), heads(decay), heads(a)
    k = heads(k) * (1.0 + (a - 1.0) * k_a.astype(f32).reshape(N_HEADS, HEAD_DIM))

    def step(state, inp):
        w_t, r_t, k_t, v_t, kk_t, ab_t = inp
        sa = jnp.einsum('bhvk,bhk->bhv', state, -kk_t)
        state = (state * w_t[:, :, None, :] + sa[..., None] * ab_t[:, :, None, :]
                 + v_t[..., None] * k_t[:, :, None, :])
        return state, jnp.einsum('bhvk,bhk->bhv', state, r_t)

    s0 = jnp.zeros((B, N_HEADS, HEAD_DIM, HEAD_DIM), f32)
    xs = tuple(jnp.moveaxis(t, 1, 0) for t in (decay, r, k, v, kk, kk * a))
    _, y = lax.scan(step, s0, xs)
    y = jnp.moveaxis(y, 0, 1)
    y = (_head_norm(y, RWKV_GN_EPS) * ln_w.astype(f32).reshape(N_HEADS, HEAD_DIM)
         + ln_b.astype(f32).reshape(N_HEADS, HEAD_DIM))
    bonus = jnp.sum(r * k * r_k.astype(f32).reshape(N_HEADS, HEAD_DIM), -1, keepdims=True) * v
    return (y + bonus).reshape(B, S, GROUP_WIDTH)


def pool_mix(u, w_grp, scale):
    f32 = jnp.float32
    B, S, _ = u.shape
    uf = u.astype(f32)
    cs = jnp.pad(jnp.cumsum(uf, axis=1), ((0, 0), (1, 0), (0, 0)))
    hi = cs[:, 1:]
    t1 = jnp.arange(1, S + 1, dtype=f32)[None, :, None]
    means = []
    for gi, win in enumerate(POOL_WINDOWS):
        sl = slice(gi * POOL_GROUP, (gi + 1) * POOL_GROUP)
        lo = jnp.pad(cs[:, :S + 1 - win, sl], ((0, 0), (win - 1, 0), (0, 0)))
        means.append((hi[..., sl] - lo) / jnp.minimum(t1, float(win)))
    d = (jnp.concatenate(means, -1) - uf).reshape(B, S, len(POOL_WINDOWS), POOL_GROUP)
    y = jnp.einsum('bsgc,gcd->bsgd', d, w_grp.astype(f32)).reshape(B, S, GROUP_WIDTH)
    return y * scale.astype(f32)


def mlstm_mix(qk, v, ig_pre, fg_pre, o_pre, conv_w, i_bias, f_bias, ln_w):
    f32 = jnp.float32
    B, S, _ = v.shape
    nc = S // CHUNK
    qk = jax.nn.silu(_causal_dwconv(qk, conv_w).astype(f32))
    q, k = jnp.split(qk, 2, axis=-1)
    chunked = lambda t: t.astype(f32).reshape(B, nc, CHUNK, N_HEADS, HEAD_DIM).transpose(0, 3, 1, 2, 4)
    gate_chunked = lambda t: t.astype(f32).reshape(B, nc, CHUNK, N_HEADS).transpose(0, 3, 1, 2)
    q, k, v = chunked(q), chunked(k) * HEAD_DIM ** -0.5, chunked(v)
    log_i = gate_chunked(ig_pre + i_bias)
    log_f = jax.nn.log_sigmoid(gate_chunked(fg_pre + f_bias))
    b = jnp.cumsum(log_f, axis=-1)
    causal = jnp.tril(jnp.ones((CHUNK, CHUNK), bool))
    d_log = jnp.where(causal, b[..., :, None] - b[..., None, :] + log_i[..., None, :], -jnp.inf)
    m_intra = jnp.max(d_log, -1)
    g_log = b[..., -1:] - b + log_i
    m_chunk = jnp.max(g_log, -1)
    wgt = jnp.exp(g_log - m_chunk[..., None])[..., None]
    c_chunk = jnp.einsum('bhclv,bhclk->bhcvk', v * wgt, k)
    n_chunk = jnp.sum(k * wgt, axis=3)
    f_chunk = b[..., -1]

    def step(carry, inp):
        C, n, m = carry
        cc, ncn, mc, fc = inp
        m_new = jnp.maximum(fc + m, mc)
        s_old = jnp.exp(fc + m - m_new)
        s_new = jnp.exp(mc - m_new)
        carry_new = (s_old[..., None, None] * C + s_new[..., None, None] * cc,
                     s_old[..., None] * n + s_new[..., None] * ncn, m_new)
        return carry_new, (C, n, m)

    init = (jnp.zeros((B, N_HEADS, HEAD_DIM, HEAD_DIM), f32),
            jnp.zeros((B, N_HEADS, HEAD_DIM), f32), jnp.zeros((B, N_HEADS), f32))
    xs = tuple(jnp.moveaxis(t, 2, 0) for t in (c_chunk, n_chunk, m_chunk, f_chunk))
    _, (c_prev, n_prev, m_prev) = lax.scan(step, init, xs)
    c_prev, n_prev, m_prev = (jnp.moveaxis(t, 0, 2) for t in (c_prev, n_prev, m_prev))
    m_inter = b + m_prev[..., None]
    m_t = jnp.maximum(m_inter, m_intra)
    s_inter = jnp.exp(m_inter - m_t)
    scores = jnp.einsum('bhcld,bhcsd->bhcls', q, k) * jnp.exp(d_log - m_t[..., None])
    num = (jnp.einsum('bhcls,bhcsv->bhclv', scores, v)
           + s_inter[..., None] * jnp.einsum('bhcvk,bhclk->bhclv', c_prev, q))
    den = jnp.sum(scores, -1) + s_inter * jnp.einsum('bhck,bhclk->bhcl', n_prev, q)
    h = num / jnp.maximum(jnp.abs(den), jnp.exp(-m_t))[..., None]
    h = h.transpose(0, 2, 3, 1, 4).reshape(B, S, GROUP_WIDTH) * jax.nn.sigmoid(o_pre.astype(f32))
    h = _head_norm(h.reshape(B, S, N_HEADS, HEAD_DIM), HEAD_NORM_EPS) * ln_w.astype(f32).reshape(N_HEADS, HEAD_DIM)
    return h.reshape(B, S, GROUP_WIDTH)


def retention_mix(q, k, v, cos, sin):
    f32 = jnp.float32
    B, S, _ = q.shape
    nc = S // CHUNK
    to_heads = lambda t: t.astype(f32).reshape(B, S, N_HEADS, HEAD_DIM)
    q = _rotary(to_heads(q), cos, sin)
    k = _rotary(to_heads(k), cos, sin) * HEAD_DIM ** -0.5
    v = to_heads(v)
    chunked = lambda t: t.reshape(B, nc, CHUNK, N_HEADS, HEAD_DIM).transpose(0, 3, 1, 2, 4)
    q, k, v = chunked(q), chunked(k), chunked(v)
    log_g = jnp.log1p(-jnp.exp2(-5.0 - jnp.arange(N_HEADS, dtype=f32)))
    pos = jnp.arange(CHUNK, dtype=f32)
    rel = pos[:, None] - pos[None, :]
    dec = jnp.where(rel >= 0, jnp.exp(log_g[:, None, None] * jnp.maximum(rel, 0.0)), 0.0)
    scores = jnp.einsum('bhcld,bhcsd->bhcls', q, k) * dec[None, :, None]
    intra = jnp.einsum('bhcls,bhcsv->bhclv', scores, v)
    zeta = jnp.exp(log_g[:, None] * (CHUNK - 1.0 - pos))
    r_chunk = jnp.einsum('bhclk,bhclv->bhckv', k * zeta[None, :, None, :, None], v)
    g_chunk = jnp.exp(log_g * CHUNK)[None, :, None, None]

    def step(R, rc):
        return g_chunk * R + rc, R

    _, r_prev = lax.scan(step, jnp.zeros((B, N_HEADS, HEAD_DIM, HEAD_DIM), f32),
                         jnp.moveaxis(r_chunk, 2, 0))
    r_prev = jnp.moveaxis(r_prev, 0, 2)
    xi = jnp.exp(log_g[:, None] * (pos + 1.0))
    inter = jnp.einsum('bhcld,bhcdv->bhclv', q, r_prev) * xi[None, :, None, :, None]
    y = (intra + inter).transpose(0, 2, 3, 1, 4).reshape(B, S, N_HEADS, HEAD_DIM)
    return _head_norm(y, HEAD_NORM_EPS).reshape(B, S, GROUP_WIDTH)


def setup_inputs(seed: int = 0) -> dict:
    key = jax.random.key(seed)
    ks = jax.random.split(key, 32)
    f32 = jnp.float32
    G = GROUP_WIDTH
    nrm = lambda k, shape, s: s * jax.random.normal(k, shape, f32)
    return {
        "x": jax.random.normal(ks[0], (BATCH, SEQ, D_MODEL), f32),
        "p": jax.random.normal(ks[1], (DEPTH, BATCH, SEQ, D_PLE), f32),
        "w_in": nrm(ks[2], (DEPTH, D_MODEL, N_COLS), D_MODEL ** -0.5),
        "rw_mu": jax.random.uniform(ks[3], (DEPTH, RWKV_SHIFT_COLS), f32),
        "rw_w0": jax.random.uniform(ks[4], (DEPTH, G), f32, -5.0, 0.0),
        "rw_w2": nrm(ks[5], (DEPTH, RWKV_W_LORA, G), 0.1 * RWKV_W_LORA ** -0.5),
        "rw_a0": nrm(ks[6], (DEPTH, G), 0.1),
        "rw_a2": nrm(ks[7], (DEPTH, RWKV_A_LORA, G), RWKV_A_LORA ** -0.5),
        "rw_kk": 0.85 + nrm(ks[8], (DEPTH, G), 0.02),
        "rw_ka": 1.0 + nrm(ks[9], (DEPTH, G), 0.02),
        "rw_rk": nrm(ks[10], (DEPTH, G), 0.1),
        "rw_ln_w": 1.0 + nrm(ks[11], (DEPTH, G), 0.02),
        "rw_ln_b": nrm(ks[12], (DEPTH, G), 0.02),
        "pl_w": nrm(ks[13], (DEPTH, len(POOL_WINDOWS), POOL_GROUP, POOL_GROUP), POOL_GROUP ** -0.5),
        "pl_scale": 1.0 + nrm(ks[14], (DEPTH, G), 0.02),
        "ml_conv": nrm(ks[15], (DEPTH, MLSTM_CONV, 2 * G), MLSTM_CONV ** -0.5),
        "ml_ib": nrm(ks[16], (DEPTH, N_HEADS), 0.1),
        "ml_fb": jnp.linspace(3.0, 6.0, N_HEADS, dtype=f32)[None, :] + nrm(ks[17], (DEPTH, N_HEADS), 0.1),
        "ml_ln_w": 1.0 + nrm(ks[18], (DEPTH, G), 0.02),
        "w_out": nrm(ks[19], (DEPTH, D_MODEL, D_MODEL), D_MODEL ** -0.5 * DEEPNORM_BETA),
        "ple_w": nrm(ks[20], (DEPTH, D_PLE, D_MODEL), D_PLE ** -0.5 * DEEPNORM_BETA),
        "ple_gate": nrm(ks[21], (DEPTH, D_MODEL, D_MODEL), D_MODEL ** -0.5),
        "ln_g": 1.0 + nrm(ks[22], (DEPTH, D_MODEL), 0.02),
        "ln_b": nrm(ks[23], (DEPTH, D_MODEL), 0.02),
    }


def reference(x, p, w_in, rw_mu, rw_w0, rw_w2, rw_a0, rw_a2, rw_kk, rw_ka, rw_rk, rw_ln_w,
              rw_ln_b, pl_w, pl_scale, ml_conv, ml_ib, ml_fb, ml_ln_w, w_out, ple_w, ple_gate,
              ln_g, ln_b):
    dtype = x.dtype
    S = x.shape[1]
    pos = jnp.arange(S, dtype=jnp.float32)
    inv_freq = ROPE_BASE ** (-jnp.arange(0, HEAD_DIM, 2, dtype=jnp.float32) / HEAD_DIM)
    ang = pos[:, None] * inv_freq[None, :]
    cos, sin = jnp.cos(ang), jnp.sin(ang)
    for i in range(DEPTH):
        z = x @ w_in[i]
        (rw_in, rw_g, pl_u, pl_g, ml_qk, ml_v, ml_i, ml_f, ml_o, ml_g,
         rt_q, rt_k, rt_v, rt_g) = _split_columns(z)
        y_a = rwkv7_mix(rw_in, rw_mu[i], rw_w0[i], rw_w2[i], rw_a0[i], rw_a2[i], rw_kk[i],
                        rw_ka[i], rw_rk[i], rw_ln_w[i], rw_ln_b[i]) * jax.nn.silu(rw_g)
        y_b = pool_mix(pl_u, pl_w[i], pl_scale[i]) * jax.nn.silu(pl_g)
        y_c = mlstm_mix(ml_qk, ml_v, ml_i, ml_f, ml_o, ml_conv[i], ml_ib[i], ml_fb[i],
                        ml_ln_w[i]) * jax.nn.silu(ml_g)
        y_d = retention_mix(rt_q, rt_k, rt_v, cos, sin) * jax.nn.silu(rt_g)
        mixed = jnp.concatenate([y_a, y_b, y_c, y_d], axis=-1).astype(dtype)
        ple = jax.nn.sigmoid(x @ ple_gate[i]) * (p[i] @ ple_w[i])
        x = _layer_norm(DEEPNORM_ALPHA * x + mixed @ w_out[i] + ple, ln_g[i], ln_b[i], LN_EPS).astype(dtype)
    return x
```

```python
import functools

import jax
import jax.numpy as jnp
import numpy as np
from jax import lax
from jax.experimental import pallas as pl
from jax.experimental.pallas import tpu as pltpu

D_MODEL = 1024
N_HEADS = 4
HEAD_DIM = 64
GROUP = 256
CHUNK = 64
LORA = 64
POOL_WINDOWS = (2, 4, 8, 16)
POOL_HIST = 16
CONV_K = 4
HIST = 8
D_PLE = 256
ROPE_BASE = 10000.0
LN_EPS = 1e-5
HEAD_NORM_EPS = 1e-6
RWKV_GN_EPS = 64e-5
RWKV_SHIFT = 3 * GROUP + 2 * LORA
GATE_PAD = 128

C_RW = 0
C_RWG = C_RW + RWKV_SHIFT
C_PLU = C_RWG + GROUP
C_PLG = C_PLU + GROUP
C_MLQK = C_PLG + GROUP
C_MLV = C_MLQK + 2 * GROUP
C_MLO = C_MLV + GROUP
C_MLG = C_MLO + GROUP
C_RTQ = C_MLG + GROUP
C_RTK = C_RTQ + GROUP
C_RTV = C_RTK + GROUP
C_RTG = C_RTV + GROUP
C_GATE = C_RTG + GROUP
N_COLS_PACKED = C_GATE + GATE_PAD

TILE_T = 512

F32 = jnp.float32
BF16 = jnp.bfloat16


def _dot(a, b):
    return jnp.dot(a.astype(BF16), b.astype(BF16), preferred_element_type=F32)


def _nt(a, b):
    return lax.dot_general(a.astype(BF16), b.astype(BF16), (((1,), (1,)), ((), ())),
                           preferred_element_type=F32)


def _tn(a, b):
    return lax.dot_general(a.astype(BF16), b.astype(BF16), (((0,), (0,)), ((), ())),
                           preferred_element_type=F32)


def _split3(x):
    hi = x.astype(BF16)
    r1 = x - hi.astype(F32)
    mid = r1.astype(BF16)
    lo = (r1 - mid.astype(F32)).astype(BF16)
    return hi, mid, lo


def _dot_exact_lhs(m, x):
    hi, mid, lo = _split3(x)
    d = lambda t: jnp.dot(m, t, preferred_element_type=F32)
    return d(hi) + d(mid) + d(lo)


def _nt_exact_lhs(m, x):
    hi, mid, lo = _split3(x)
    d = lambda t: lax.dot_general(m, t, (((1,), (1,)), ((), ())), preferred_element_type=F32)
    return d(hi) + d(mid) + d(lo)


def _segsum(x, seg):
    hi = x.astype(BF16)
    lo = (x - hi.astype(F32)).astype(BF16)
    return (jnp.dot(hi, seg, preferred_element_type=F32)
            + jnp.dot(lo, seg, preferred_element_type=F32))


def _head_norm(y, seg, eps):
    mu = _segsum(y, seg) * (1.0 / HEAD_DIM)
    yc = y - mu
    var = _segsum(yc * yc, seg) * (1.0 / HEAD_DIM)
    return yc * lax.rsqrt(var + eps)


def _softplus(x):
    return jnp.maximum(x, 0.0) + jnp.log(1.0 + jnp.exp(-jnp.abs(x)))


def _sigmoid(x):
    return 1.0 / (1.0 + jnp.exp(-x))


def _silu(x):
    return x * _sigmoid(x)


def _layer_kernel(
        x_ref, p_ref, w_in_ref, w_out_ref, ple_gate_ref, ple_w_ref,
        mu_ref, rwp_ref, lora_ref, plw_ref, conv_ref, gbias_ref, mlln_ref, lnp_ref,
        cos_ref, sin_ref, dec_ref, rtp_ref, seg_ref, ltri_ref,
        o_ref,
        zbuf, cbuf, ubuf,
        rw_r, rw_k, rw_v, rw_lw, rw_kk, rw_b, rw_y, rw_state,
        ml_q, ml_k, ml_v, ml_gt, ml_y, ml_state, ml_m,
        rt_q, rt_qx, rt_k, rt_kz, rt_v, rt_y, rt_state,
        mixed,
        *, alpha):
    T = x_ref.shape[0]
    n_chunks = T // CHUNK
    t_idx = pl.program_id(1)

    @pl.when(t_idx == 0)
    def _reset():
        zbuf[0:HIST, :] = jnp.zeros((HIST, RWKV_SHIFT), F32)
        cbuf[0:HIST, :] = jnp.zeros((HIST, 2 * GROUP), F32)
        ubuf[0:POOL_HIST, :] = jnp.zeros((POOL_HIST, GROUP), F32)
        rw_state[...] = jnp.zeros_like(rw_state)
        ml_state[...] = jnp.zeros_like(ml_state)
        ml_m[...] = jnp.zeros_like(ml_m)
        rt_state[...] = jnp.zeros_like(rt_state)

    x = x_ref[...]
    xb = x.astype(BF16)
    seg = seg_ref[...]
    ltri = ltri_ref[...]

    def proj(c0, width):
        return jnp.dot(xb, w_in_ref[:, c0:c0 + width], preferred_element_type=F32)

    zrw = proj(C_RW, RWKV_SHIFT)
    zbuf[HIST:HIST + T, :] = zrw
    prev = zbuf[HIST - 1:HIST - 1 + T, :]
    zs = zrw + (prev - zrw) * mu_ref[...]
    zbuf[0:HIST, :] = zbuf[T:T + HIST, :]
    r = zs[:, 0:GROUP]
    k = zs[:, GROUP:2 * GROUP]
    v = zs[:, 2 * GROUP:3 * GROUP]
    wa = zs[:, 3 * GROUP:RWKV_SHIFT]
    lane = lax.broadcasted_iota(jnp.int32, wa.shape, 1)
    wa = jnp.where(lane < LORA, jnp.tanh(wa), wa)
    lo = _dot(wa, lora_ref[...])
    rwp = rwp_ref[...]
    w0, a0, k_k, k_a, r_k, rw_ln_w, rw_ln_b, pl_scale = (rwp[i:i + 1, :] for i in range(8))
    w_log = -_softplus(-(w0 + lo[:, 0:GROUP])) - 0.5
    a = _sigmoid(a0 + lo[:, GROUP:2 * GROUP])
    kk = k * k_k
    kk = kk / jnp.maximum(jnp.sqrt(_segsum(kk * kk, seg)), 1e-12)
    k2 = k * (1.0 + (a - 1.0) * k_a)
    rw_r[...] = r
    rw_k[...] = k2
    rw_v[...] = v
    rw_lw[...] = -jnp.exp(w_log)
    rw_kk[...] = kk
    rw_b[...] = kk * a
    bonus = _segsum(r * k2 * r_k, seg) * v

    zqk = proj(C_MLQK, 2 * GROUP)
    cbuf[HIST:HIST + T, :] = zqk
    cw = conv_ref[...]
    conv = zqk * cw[CONV_K - 1:CONV_K, :]
    for j in range(1, CONV_K):
        conv = conv + cbuf[HIST - j:HIST - j + T, :] * cw[CONV_K - 1 - j:CONV_K - j, :]
    cbuf[0:HIST, :] = cbuf[T:T + HIST, :]
    qk = _silu(conv)
    ml_q[...] = qk[:, 0:GROUP]
    ml_k[...] = qk[:, GROUP:2 * GROUP] * (HEAD_DIM ** -0.5)
    ml_v[...] = proj(C_MLV, GROUP)
    zg = proj(C_GATE, GATE_PAD) + gbias_ref[...]
    glane = lax.broadcasted_iota(jnp.int32, zg.shape, 1)
    ml_gt[...] = jnp.where(glane < N_HEADS, zg,
                           jnp.where(glane < 2 * N_HEADS, -_softplus(-zg), 0.0))

    cos = cos_ref[...]
    sin = sin_ref[...]
    cos2 = jnp.concatenate([cos, cos], axis=1)
    sin2 = jnp.concatenate([sin, sin], axis=1)
    hl = lax.broadcasted_iota(jnp.int32, (T, GROUP), 1) % HEAD_DIM

    def rotary(z):
        swapped = jnp.where(hl < HEAD_DIM // 2,
                            pltpu.roll(z, GROUP - HEAD_DIM // 2, axis=1),
                            pltpu.roll(z, HEAD_DIM // 2, axis=1))
        return z * cos2 + swapped * sin2

    rtp = rtp_ref[...]
    q_rot = rotary(proj(C_RTQ, GROUP))
    k_rot = rotary(proj(C_RTK, GROUP)) * (HEAD_DIM ** -0.5)
    rt_q[...] = q_rot
    rt_k[...] = k_rot
    rt_v[...] = proj(C_RTV, GROUP)
    for c in range(n_chunks):
        rows = slice(c * CHUNK, (c + 1) * CHUNK)
        rt_qx[rows, :] = q_rot[rows, :] * rtp[:, 0:GROUP]
        rt_kz[rows, :] = k_rot[rows, :] * rtp[:, GROUP:2 * GROUP]

    ri = lax.broadcasted_iota(jnp.int32, (CHUNK, CHUNK), 0)
    ci = lax.broadcasted_iota(jnp.int32, (CHUNK, CHUNK), 1)
    tri_s = ri > ci
    tri_i = ri >= ci
    eye = jnp.where(ri == ci, 1.0, 0.0).astype(F32)
    e8 = jnp.where(lax.broadcasted_iota(jnp.int32, (8, GATE_PAD), 0)
                   == lax.broadcasted_iota(jnp.int32, (8, GATE_PAD), 1), 1.0, 0.0).astype(BF16)
    ones_h = jnp.ones((CHUNK, HEAD_DIM), F32)
    g_chunk = rtp[0:1, 2 * GROUP:3 * GROUP]
    dec = dec_ref[...]

    def chunk_body(c, carry):
        rows = pl.ds(pl.multiple_of(c * CHUNK, CHUNK), CHUNK)

        lw = rw_lw[rows, :]
        cum = _dot_exact_lhs(ltri, lw)
        g_l = cum[CHUNK - 1:CHUNK, :]
        e_neg = jnp.exp(-cum)
        kk_c = rw_kk[rows, :]
        b_c = rw_b[rows, :]
        k_c = rw_k[rows, :]
        v_c = rw_v[rows, :]
        al_c = jnp.exp(cum - lw) * kk_c
        be_c = b_c * e_neg
        kt_c = k_c * e_neg
        rt_c = rw_r[rows, :] * jnp.exp(cum)
        e_g = jnp.exp(g_l - cum)
        kh_c = k_c * e_g
        bh_c = b_c * e_g
        dec_l = jnp.exp(g_l)
        ys = []
        for h in range(N_HEADS):
            hs = slice(h * HEAD_DIM, (h + 1) * HEAD_DIM)
            al, be, kt, rt = al_c[:, hs], be_c[:, hs], kt_c[:, hs], rt_c[:, hs]
            vh = v_c[:, hs]
            n_m = jnp.where(tri_s, _nt(al, be), 0.0)
            p_m = jnp.where(tri_s, _nt(al, kt), 0.0)
            q_m = jnp.where(tri_i, _nt(rt, kt), 0.0)
            rb_m = jnp.where(tri_i, _nt(rt, be), 0.0)
            tinv = eye - n_m
            pw = n_m
            for _ in range(5):
                pw = _dot(pw, pw)
                tinv = tinv + _dot(tinv, pw)
            st = rw_state[h]
            u = _dot(tinv, _nt(al, st) + _dot(p_m, vh))
            ys.append(_nt(rt, st) + _dot(q_m, vh) - _dot(rb_m, u))
            rw_state[h] = st * dec_l[:, hs] + _tn(vh, kh_c[:, hs]) - _tn(u, bh_c[:, hs])
        rw_y[rows, :] = jnp.concatenate(ys, axis=1)

        gt = ml_gt[rows, :]
        bcum = _dot_exact_lhs(ltri, gt)
        gl2 = lax.broadcasted_iota(jnp.int32, gt.shape, 1)
        gb = jnp.where(gl2 < N_HEADS, gt, bcum)
        gb_t = _nt_exact_lhs(e8, gb)
        q_c = ml_q[rows, :]
        k_c2 = ml_k[rows, :]
        v_c2 = ml_v[rows, :]
        ys = []
        for h in range(N_HEADS):
            hs = slice(h * HEAD_DIM, (h + 1) * HEAD_DIM)
            b_col = gb[:, N_HEADS + h:N_HEADS + h + 1]
            li_col = gb[:, h:h + 1]
            b_row = gb_t[N_HEADS + h:N_HEADS + h + 1, :]
            li_row = gb_t[h:h + 1, :]
            b_last = b_row[:, CHUNK - 1:CHUNK]
            d_log = jnp.where(tri_i, b_col - b_row + li_row, -jnp.inf)
            m_intra = jnp.max(d_log, axis=-1, keepdims=True)
            m_chunk = jnp.max(b_last - b_row + li_row, axis=-1, keepdims=True)
            wgt = jnp.exp(b_last - b_col + li_col - m_chunk)
            m_prev = ml_m[h:h + 1, 0:1]
            m_inter = b_col + m_prev
            m_t = jnp.maximum(m_inter, m_intra)
            s_inter = jnp.exp(m_inter - m_t)
            qh, kh, vh = q_c[:, hs], k_c2[:, hs], v_c2[:, hs]
            va = jnp.concatenate([vh, ones_h], axis=1)
            scores = _nt(qh, kh) * jnp.exp(d_log - m_t)
            st = ml_state[h]
            tot = _dot(scores, va) + s_inter * _dot(qh, st)
            den = tot[:, HEAD_DIM:2 * HEAD_DIM]
            ys.append(tot[:, 0:HEAD_DIM] / jnp.maximum(jnp.abs(den), jnp.exp(-m_t)))
            m_new = jnp.maximum(b_last + m_prev, m_chunk)
            s_old = jnp.exp(b_last + m_prev - m_new)
            s_new = jnp.exp(m_chunk - m_new)
            ml_state[h] = s_old * st + s_new * _tn(kh, va * wgt)
            ml_m[h:h + 1, :] = jnp.broadcast_to(m_new, (1, ml_m.shape[1]))
        ml_y[rows, :] = jnp.concatenate(ys, axis=1)

        q_c3 = rt_q[rows, :]
        qx_c = rt_qx[rows, :]
        k_c3 = rt_k[rows, :]
        kz_c = rt_kz[rows, :]
        v_c3 = rt_v[rows, :]
        ys = []
        for h in range(N_HEADS):
            hs = slice(h * HEAD_DIM, (h + 1) * HEAD_DIM)
            vh = v_c3[:, hs]
            scores = _nt(q_c3[:, hs], k_c3[:, hs]) * dec[:, hs]
            st = rt_state[h]
            ys.append(_dot(scores, vh) + _dot(qx_c[:, hs], st))
            rt_state[h] = g_chunk[:, hs] * st + _tn(kz_c[:, hs], vh)
        rt_y[rows, :] = jnp.concatenate(ys, axis=1)
        return carry

    lax.fori_loop(0, n_chunks, chunk_body, 0)

    y_a = (_head_norm(rw_y[...], seg, RWKV_GN_EPS) * rw_ln_w + rw_ln_b + bonus)
    mixed[:, 0:GROUP] = y_a * _silu(proj(C_RWG, GROUP))

    u = proj(C_PLU, GROUP)
    ubuf[POOL_HIST:POOL_HIST + T, :] = u
    pos = (t_idx * T + lax.broadcasted_iota(jnp.int32, (T, GROUP), 0) + 1).astype(F32)
    pgrp = lax.broadcasted_iota(jnp.int32, (T, GROUP), 1) // (GROUP // len(POOL_WINDOWS))
    acc = u
    mean = jnp.zeros((T, GROUP), F32)
    j = 1
    for gi, win in enumerate(POOL_WINDOWS):
        while j < win:
            acc = acc + ubuf[POOL_HIST - j:POOL_HIST - j + T, :]
            j += 1
        mean = jnp.where(pgrp == gi, acc / jnp.minimum(pos, float(win)), mean)
    ubuf[0:POOL_HIST, :] = ubuf[T:T + POOL_HIST, :]
    y_b = _dot(mean - u, plw_ref[...]) * pl_scale
    mixed[:, GROUP:2 * GROUP] = y_b * _silu(proj(C_PLG, GROUP))

    hcm = ml_y[...] * _sigmoid(proj(C_MLO, GROUP))
    y_c = _head_norm(hcm, seg, HEAD_NORM_EPS) * mlln_ref[...]
    mixed[:, 2 * GROUP:3 * GROUP] = y_c * _silu(proj(C_MLG, GROUP))

    y_d = _head_norm(rt_y[...], seg, HEAD_NORM_EPS)
    mixed[:, 3 * GROUP:4 * GROUP] = y_d * _silu(proj(C_RTG, GROUP))

    ple = _sigmoid(jnp.dot(xb, ple_gate_ref[...], preferred_element_type=F32)) * _dot(p_ref[...], ple_w_ref[...])
    hres = alpha * x + _dot(mixed[...], w_out_ref[...]) + ple
    lnp = lnp_ref[...]
    mu_ln = jnp.mean(hres, axis=-1, keepdims=True)
    hc = hres - mu_ln
    var = jnp.mean(hc * hc, axis=-1, keepdims=True)
    o_ref[...] = hc * lax.rsqrt(var + LN_EPS) * lnp[0:1, :] + lnp[1:2, :]


def _full(shape):
    return pl.BlockSpec(shape, lambda b, t: (0,) * len(shape))


def _layer_call(x, p, consts, *, alpha, tile_t):
    B, S, _ = x.shape
    T = tile_t
    grid = (B, S // T)
    tile = lambda width: pl.BlockSpec((None, T, width), lambda b, t: (b, t, 0))
    in_specs = [tile(D_MODEL), tile(D_PLE)]
    names = ["w_in", "w_out", "ple_gate", "ple_w", "mu", "rwp", "lora", "plw", "conv", "gbias",
             "mlln", "lnp"]
    in_specs += [_full(consts[n].shape) for n in names]
    in_specs += [pl.BlockSpec((T, 2 * HEAD_DIM), lambda b, t: (t, 0))] * 2
    tail = ["dec", "rtp", "seg", "ltri"]
    in_specs += [_full(consts[n].shape) for n in tail]
    act = lambda width: pltpu.VMEM((T, width), F32)
    scratch = [
        pltpu.VMEM((T + HIST, RWKV_SHIFT), F32),
        pltpu.VMEM((T + HIST, 2 * GROUP), F32),
        pltpu.VMEM((T + POOL_HIST, GROUP), F32),
        act(GROUP), act(GROUP), act(GROUP), act(GROUP), act(GROUP), act(GROUP), act(GROUP),
        pltpu.VMEM((N_HEADS, HEAD_DIM, HEAD_DIM), F32),
        act(GROUP), act(GROUP), act(GROUP), act(GATE_PAD), act(GROUP),
        pltpu.VMEM((N_HEADS, HEAD_DIM, 2 * HEAD_DIM), F32),
        pltpu.VMEM((8, 128), F32),
        act(GROUP), act(GROUP), act(GROUP), act(GROUP), act(GROUP), act(GROUP),
        pltpu.VMEM((N_HEADS, HEAD_DIM, HEAD_DIM), F32),
        act(D_MODEL),
    ]
    args = [x, p] + [consts[n] for n in names] + [consts["cos"], consts["sin"]] + [consts[n] for n in tail]
    return pl.pallas_call(
        functools.partial(_layer_kernel, alpha=alpha),
        grid=grid,
        in_specs=in_specs,
        out_specs=tile(D_MODEL),
        out_shape=jax.ShapeDtypeStruct((B, S, D_MODEL), F32),
        scratch_shapes=scratch,
        compiler_params=pltpu.CompilerParams(
            dimension_semantics=("arbitrary", "arbitrary"),
            vmem_limit_bytes=56 * 1024 * 1024),
        name="deepnorm_mixer_layer",
    )(*args)


def _pack_w_in(w_in):
    g0 = 2432
    body = jnp.concatenate([w_in[:, :g0], w_in[:, g0 + 2 * N_HEADS:]], axis=1)
    perm = np.arange(body.shape[1])
    deint = np.concatenate([np.arange(0, HEAD_DIM, 2), np.arange(1, HEAD_DIM, 2)])
    for c0 in (C_RTQ, C_RTK):
        for h in range(N_HEADS):
            base = c0 + h * HEAD_DIM
            perm[base:base + HEAD_DIM] = base + deint
    body = body[:, perm]
    gates = jnp.pad(w_in[:, g0:g0 + 2 * N_HEADS], ((0, 0), (0, GATE_PAD - 2 * N_HEADS)))
    return jnp.concatenate([body, gates], axis=1).astype(BF16)


def _block_diag(blocks):
    n = len(blocks)
    rows = []
    for i, blk in enumerate(blocks):
        z = jnp.zeros_like(blk)
        rows.append(jnp.concatenate([blk if j == i else z for j in range(n)], axis=1))
    return jnp.concatenate(rows, axis=0)


def kernel(x, p, w_in, rw_mu, rw_w0, rw_w2, rw_a0, rw_a2, rw_kk, rw_ka, rw_rk, rw_ln_w, rw_ln_b, pl_w, pl_scale, ml_conv, ml_ib, ml_fb, ml_ln_w, w_out, ple_w, ple_gate, ln_g, ln_b):
    B, S, D = x.shape
    depth = w_in.shape[0]
    alpha = (2.0 * depth) ** 0.25
    tile_t = min(TILE_T, S)
    assert D == D_MODEL and S % tile_t == 0 and tile_t % CHUNK == 0

    pos = jnp.arange(S, dtype=F32)
    inv_freq = ROPE_BASE ** (-jnp.arange(0, HEAD_DIM, 2, dtype=F32) / HEAD_DIM)
    ang = pos[:, None] * inv_freq[None, :]
    cos, sin = jnp.cos(ang), jnp.sin(ang)
    cos_t = jnp.concatenate([cos, cos, cos, cos], axis=1)
    sin_t = jnp.concatenate([-sin, sin, -sin, sin], axis=1)
    log_g = jnp.log1p(-jnp.exp2(-5.0 - jnp.arange(N_HEADS, dtype=F32)))
    cpos = jnp.arange(CHUNK, dtype=F32)
    rel = cpos[:, None] - cpos[None, :]
    dec = jnp.where(rel >= 0, jnp.exp(log_g[:, None, None] * jnp.maximum(rel, 0.0)), 0.0)
    dec = dec.transpose(1, 0, 2).reshape(CHUNK, GROUP)
    zeta = jnp.exp(log_g[:, None] * (CHUNK - 1.0 - cpos))
    xi = jnp.exp(log_g[:, None] * (cpos + 1.0))
    g_chunk = jnp.exp(log_g * CHUNK)
    lanes = lambda t: jnp.repeat(t.T, HEAD_DIM, axis=1)
    rtp = jnp.concatenate([lanes(xi), lanes(zeta),
                           jnp.broadcast_to(jnp.repeat(g_chunk, HEAD_DIM)[None, :], (CHUNK, GROUP))], axis=1)
    head_of = np.arange(GROUP) // HEAD_DIM
    seg = jnp.asarray(head_of[:, None] == head_of[None, :], BF16)
    ltri = jnp.asarray(np.tril(np.ones((CHUNK, CHUNK))), BF16)

    for i in range(depth):
        row = lambda t: t.reshape(1, -1).astype(F32)
        consts = {
            "w_in": _pack_w_in(w_in[i]),
            "w_out": w_out[i].astype(BF16),
            "ple_gate": ple_gate[i].astype(BF16),
            "ple_w": ple_w[i].astype(BF16),
            "mu": row(rw_mu[i]),
            "rwp": jnp.stack([rw_w0[i], rw_a0[i], rw_kk[i], rw_ka[i], rw_rk[i], rw_ln_w[i],
                              rw_ln_b[i], pl_scale[i]]).astype(F32),
            "lora": _block_diag([rw_w2[i], rw_a2[i]]).astype(BF16),
            "plw": _block_diag([pl_w[i, g] for g in range(len(POOL_WINDOWS))]).astype(BF16),
            "conv": jnp.pad(ml_conv[i].astype(F32), ((0, 8 - CONV_K), (0, 0))),
            "gbias": jnp.pad(jnp.concatenate([ml_ib[i], ml_fb[i]]).astype(F32),
                             (0, GATE_PAD - 2 * N_HEADS)).reshape(1, GATE_PAD),
            "mlln": row(ml_ln_w[i]),
            "lnp": jnp.stack([ln_g[i], ln_b[i]]).astype(F32),
            "cos": cos_t, "sin": sin_t, "dec": dec, "rtp": rtp, "seg": seg, "ltri": ltri,
        }
        x = _layer_call(x, p[i], consts, alpha=alpha, tile_t=tile_t)
    return x
```

```python
import functools

import jax
import jax.numpy as jnp
import numpy as np
from jax import lax
from jax.experimental import pallas as pl
from jax.experimental.pallas import tpu as pltpu

D_MODEL = 1024
N_HEADS = 4
HEAD_DIM = 64
GROUP = 256
CHUNK = 64
LORA = 64
POOL_WINDOWS = (2, 4, 8, 16)
POOL_HIST = 16
CONV_K = 4
HIST = 8
D_PLE = 256
ROPE_BASE = 10000.0
LN_EPS = 1e-5
HEAD_NORM_EPS = 1e-6
RWKV_GN_EPS = 64e-5
RWKV_SHIFT = 3 * GROUP + 2 * LORA
GATE_PAD = 128

C_RW = 0
C_RWG = C_RW + RWKV_SHIFT
C_PLU = C_RWG + GROUP
C_PLG = C_PLU + GROUP
C_MLQK = C_PLG + GROUP
C_MLV = C_MLQK + 2 * GROUP
C_MLO = C_MLV + GROUP
C_MLG = C_MLO + GROUP
C_RTQ = C_MLG + GROUP
C_RTK = C_RTQ + GROUP
C_RTV = C_RTK + GROUP
C_RTG = C_RTV + GROUP
C_GATE = C_RTG + GROUP
N_COLS_PACKED = C_GATE + GATE_PAD

TILE_T = 512
PASS1_CHUNKS = 2
PASS3_CHUNKS = 2

F32 = jnp.float32
BF16 = jnp.bfloat16


def _dot(a, b):
    return jnp.dot(a.astype(BF16), b.astype(BF16), preferred_element_type=F32)


def _nt(a, b):
    return lax.dot_general(a.astype(BF16), b.astype(BF16), (((1,), (1,)), ((), ())),
                           preferred_element_type=F32)


def _tn(a, b):
    return lax.dot_general(a.astype(BF16), b.astype(BF16), (((0,), (0,)), ((), ())),
                           preferred_element_type=F32)


def _split3(x):
    hi = x.astype(BF16)
    r1 = x - hi.astype(F32)
    mid = r1.astype(BF16)
    lo = (r1 - mid.astype(F32)).astype(BF16)
    return hi, mid, lo


def _dot_exact_lhs(m, x):
    hi, mid, lo = _split3(x)
    d = lambda t: jnp.dot(m, t, preferred_element_type=F32)
    return d(hi) + d(mid) + d(lo)


def _nt_exact_lhs(m, x):
    hi, mid, lo = _split3(x)
    d = lambda t: lax.dot_general(m, t, (((1,), (1,)), ((), ())), preferred_element_type=F32)
    return d(hi) + d(mid) + d(lo)


def _segsum(x, seg):
    hi = x.astype(BF16)
    lo = (x - hi.astype(F32)).astype(BF16)
    return (jnp.dot(hi, seg, preferred_element_type=F32)
            + jnp.dot(lo, seg, preferred_element_type=F32))


def _head_norm(y, seg, eps):
    mu = _segsum(y, seg) * (1.0 / HEAD_DIM)
    yc = y - mu
    var = _segsum(yc * yc, seg) * (1.0 / HEAD_DIM)
    return yc * lax.rsqrt(var + eps)


def _softplus(x):
    return jnp.maximum(x, 0.0) + jnp.log(1.0 + jnp.exp(-jnp.abs(x)))


def _sigmoid(x):
    return 1.0 / (1.0 + jnp.exp(-x))


def _silu(x):
    return x * _sigmoid(x)


def _layer_kernel(
        x_ref, p_ref, w_in_ref, w_out_ref, ple_gate_ref, ple_w_ref,
        mu_ref, rwp_ref, lora_ref, plw_ref, conv_ref, gbias_ref, mlln_ref, lnp_ref,
        cos_ref, sin_ref, dec_ref, rtp_ref, seg_ref, ltri_ref,
        o_ref,
        zbuf, cbuf, ubuf,
        rw_r, rw_k, rw_v, rw_lw, rw_kk, rw_b, rw_y, rw_state,
        rw_rw, rw_g, rw_h, rw_dec,
        ml_q, ml_k, ml_v, ml_gt, ml_y, ml_state, ml_m,
        ml_c, ml_gb, ml_gbt, ml_sc,
        rt_q, rt_qx, rt_k, rt_kz, rt_v, rt_y, rt_state,
        rt_r,
        mixed,
        *, alpha):
    T = x_ref.shape[0]
    n_chunks = T // CHUNK
    t_idx = pl.program_id(1)

    @pl.when(t_idx == 0)
    def _reset():
        zbuf[0:HIST, :] = jnp.zeros((HIST, RWKV_SHIFT), F32)
        cbuf[0:HIST, :] = jnp.zeros((HIST, 2 * GROUP), F32)
        ubuf[0:POOL_HIST, :] = jnp.zeros((POOL_HIST, GROUP), F32)
        rw_state[...] = jnp.zeros_like(rw_state)
        ml_state[...] = jnp.zeros_like(ml_state)
        ml_m[...] = jnp.zeros_like(ml_m)
        rt_state[...] = jnp.zeros_like(rt_state)

    x = x_ref[...]
    xb = x.astype(BF16)
    seg = seg_ref[...]
    ltri = ltri_ref[...]

    def proj(c0, width):
        return jnp.dot(xb, w_in_ref[:, c0:c0 + width], preferred_element_type=F32)

    zrw = proj(C_RW, RWKV_SHIFT)
    zbuf[HIST:HIST + T, :] = zrw
    prev = zbuf[HIST - 1:HIST - 1 + T, :]
    zs = zrw + (prev - zrw) * mu_ref[...]
    zbuf[0:HIST, :] = zbuf[T:T + HIST, :]
    r = zs[:, 0:GROUP]
    k = zs[:, GROUP:2 * GROUP]
    v = zs[:, 2 * GROUP:3 * GROUP]
    wa = zs[:, 3 * GROUP:RWKV_SHIFT]
    lane = lax.broadcasted_iota(jnp.int32, wa.shape, 1)
    wa = jnp.where(lane < LORA, jnp.tanh(wa), wa)
    lo = _dot(wa, lora_ref[...])
    rwp = rwp_ref[...]
    w0, a0, k_k, k_a, r_k, rw_ln_w, rw_ln_b, pl_scale = (rwp[i:i + 1, :] for i in range(8))
    w_log = -_softplus(-(w0 + lo[:, 0:GROUP])) - 0.5
    a = _sigmoid(a0 + lo[:, GROUP:2 * GROUP])
    kk = k * k_k
    kk = kk / jnp.maximum(jnp.sqrt(_segsum(kk * kk, seg)), 1e-12)
    k2 = k * (1.0 + (a - 1.0) * k_a)
    rw_r[...] = r
    rw_k[...] = k2
    rw_v[...] = v
    rw_lw[...] = -jnp.exp(w_log)
    rw_kk[...] = kk
    rw_b[...] = kk * a
    bonus = _segsum(r * k2 * r_k, seg) * v

    zqk = proj(C_MLQK, 2 * GROUP)
    cbuf[HIST:HIST + T, :] = zqk
    cw = conv_ref[...]
    conv = zqk * cw[CONV_K - 1:CONV_K, :]
    for j in range(1, CONV_K):
        conv = conv + cbuf[HIST - j:HIST - j + T, :] * cw[CONV_K - 1 - j:CONV_K - j, :]
    cbuf[0:HIST, :] = cbuf[T:T + HIST, :]
    qk = _silu(conv)
    ml_q[...] = qk[:, 0:GROUP]
    ml_k[...] = qk[:, GROUP:2 * GROUP] * (HEAD_DIM ** -0.5)
    ml_v[...] = proj(C_MLV, GROUP)
    zg = proj(C_GATE, GATE_PAD) + gbias_ref[...]
    glane = lax.broadcasted_iota(jnp.int32, zg.shape, 1)
    ml_gt[...] = jnp.where(glane < N_HEADS, zg,
                           jnp.where(glane < 2 * N_HEADS, -_softplus(-zg), 0.0))

    cos = cos_ref[...]
    sin = sin_ref[...]
    cos2 = jnp.concatenate([cos, cos], axis=1)
    sin2 = jnp.concatenate([sin, sin], axis=1)
    hl = lax.broadcasted_iota(jnp.int32, (T, GROUP), 1) % HEAD_DIM

    def rotary(z):
        swapped = jnp.where(hl < HEAD_DIM // 2,
                            pltpu.roll(z, GROUP - HEAD_DIM // 2, axis=1),
                            pltpu.roll(z, HEAD_DIM // 2, axis=1))
        return z * cos2 + swapped * sin2

    rtp = rtp_ref[...]
    q_rot = rotary(proj(C_RTQ, GROUP))
    k_rot = rotary(proj(C_RTK, GROUP)) * (HEAD_DIM ** -0.5)
    rt_q[...] = q_rot
    rt_k[...] = k_rot
    rt_v[...] = proj(C_RTV, GROUP)
    for c in range(n_chunks):
        rows = slice(c * CHUNK, (c + 1) * CHUNK)
        rt_qx[rows, :] = q_rot[rows, :] * rtp[:, 0:GROUP]
        rt_kz[rows, :] = k_rot[rows, :] * rtp[:, GROUP:2 * GROUP]

    ri = lax.broadcasted_iota(jnp.int32, (CHUNK, CHUNK), 0)
    ci = lax.broadcasted_iota(jnp.int32, (CHUNK, CHUNK), 1)
    tri_s = ri > ci
    tri_i = ri >= ci
    eye = jnp.where(ri == ci, 1.0, 0.0).astype(F32)
    e8 = jnp.where(lax.broadcasted_iota(jnp.int32, (8, GATE_PAD), 0)
                   == lax.broadcasted_iota(jnp.int32, (8, GATE_PAD), 1), 1.0, 0.0).astype(BF16)
    ones_h = jnp.ones((CHUNK, HEAD_DIM), F32)
    g_chunk = rtp[0:1, 2 * GROUP:3 * GROUP]
    dec = dec_ref[...]

    def rows_of(c):
        return pl.ds(pl.multiple_of(c * CHUNK, CHUNK), CHUNK)

    def head_slices():
        return [slice(h * HEAD_DIM, (h + 1) * HEAD_DIM) for h in range(N_HEADS)]

    heads = list(enumerate(head_slices()))

    def pass1(i, carry):
        cs = [i * PASS1_CHUNKS + j for j in range(PASS1_CHUNKS)]
        chains = [(j, h, hs) for j in range(PASS1_CHUNKS) for h, hs in heads]

        prep = []
        for c in cs:
            rows = rows_of(c)
            lw = rw_lw[rows, :]
            cum = _dot_exact_lhs(ltri, lw)
            g_l = cum[CHUNK - 1:CHUNK, :]
            e_neg = jnp.exp(-cum)
            kk_c = rw_kk[rows, :]
            b_c = rw_b[rows, :]
            k_c = rw_k[rows, :]
            e_g = jnp.exp(g_l - cum)
            rw_dec[c] = jnp.broadcast_to(jnp.exp(g_l), (8, GROUP))
            prep.append(dict(al=jnp.exp(cum - lw) * kk_c, be=b_c * e_neg, kt=k_c * e_neg,
                             rt=rw_r[rows, :] * jnp.exp(cum), kh=k_c * e_g, bh=b_c * e_g,
                             v=rw_v[rows, :]))
        pick = lambda name: [prep[j][name][:, hs] for j, h, hs in chains]
        al, be, kt, rt, kh, bh, vh = (pick(n) for n in ("al", "be", "kt", "rt", "kh", "bh", "v"))
        a2 = [jnp.concatenate([a, r], axis=0) for a, r in zip(al, rt)]
        nr = [_nt(x, y) for x, y in zip(a2, be)]
        pq = [_nt(x, y) for x, y in zip(a2, kt)]
        vk = [_tn(x, y) for x, y in zip(vh, kh)]
        n_m = [jnp.where(tri_s, t[0:CHUNK], 0.0) for t in nr]
        rb_m = [jnp.where(tri_i, t[CHUNK:2 * CHUNK], 0.0) for t in nr]
        p_m = [jnp.where(tri_s, t[0:CHUNK], 0.0) for t in pq]
        q_m = [jnp.where(tri_i, t[CHUNK:2 * CHUNK], 0.0) for t in pq]
        pw = [_dot(n, n) for n in n_m]
        pqv = [_dot(jnp.concatenate([p, q], axis=0), v) for p, q, v in zip(p_m, q_m, vh)]
        tinv = [eye - n for n in n_m]

        for c in cs:
            rows = rows_of(c)
            gt = ml_gt[rows, :]
            bcum = _dot_exact_lhs(ltri, gt)
            gl2 = lax.broadcasted_iota(jnp.int32, gt.shape, 1)
            gb = jnp.where(gl2 < N_HEADS, gt, bcum)
            gb_t = _nt_exact_lhs(e8, gb)
            ml_gb[rows, :] = gb
            ml_gbt[c] = gb_t
            b_last4 = gb_t[N_HEADS:2 * N_HEADS, CHUNK - 1:CHUNK]
            m_chunk4 = jnp.max(b_last4 - gb_t[N_HEADS:2 * N_HEADS, :] + gb_t[0:N_HEADS, :],
                               axis=-1, keepdims=True)
            ml_sc[c] = jnp.concatenate([jnp.broadcast_to(m_chunk4, (N_HEADS, 128)),
                                        jnp.broadcast_to(b_last4, (N_HEADS, 128))], axis=0)
            k_c2 = ml_k[rows, :]
            v_c2 = ml_v[rows, :]
            kz_c = rt_kz[rows, :]
            v_c3 = rt_v[rows, :]
            for h, hs in heads:
                b_col = gb[:, N_HEADS + h:N_HEADS + h + 1]
                li_col = gb[:, h:h + 1]
                wgt = jnp.exp(b_last4[h:h + 1] - b_col + li_col - m_chunk4[h:h + 1])
                va = jnp.concatenate([v_c2[:, hs], ones_h], axis=1)
                ml_c[c, h] = _tn(k_c2[:, hs], va * wgt)
                rt_r[c, h] = _tn(kz_c[:, hs], v_c3[:, hs])

        for _ in range(4):
            pw_next = [_dot(p, p) for p in pw]
            tinv = [t + _dot(t, p) for t, p in zip(tinv, pw)]
            pw = pw_next
        tinv = [t + _dot(t, p) for t, p in zip(tinv, pw)]
        wu = [_dot(t, jnp.concatenate([a, x[0:CHUNK]], axis=1))
              for t, a, x in zip(tinv, al, pqv)]
        rbwu = [_dot(x, y) for x, y in zip(rb_m, wu)]
        gu = [_tn(x, y) for x, y in zip(wu, bh)]
        for n, (j, h, hs) in enumerate(chains):
            rw_g[cs[j], h] = gu[n][0:HEAD_DIM]
            rw_h[cs[j], h] = vk[n] - gu[n][HEAD_DIM:2 * HEAD_DIM]
        for j, c in enumerate(cs):
            rows = rows_of(c)
            mine = [n for n, ch in enumerate(chains) if ch[0] == j]
            rw_rw[rows, :] = jnp.concatenate([rt[n] - rbwu[n][:, 0:HEAD_DIM] for n in mine], axis=1)
            rw_y[rows, :] = jnp.concatenate(
                [pqv[n][CHUNK:2 * CHUNK] - rbwu[n][:, HEAD_DIM:2 * HEAD_DIM] for n in mine], axis=1)
        return carry

    lax.fori_loop(0, n_chunks // PASS1_CHUNKS, pass1, 0)

    s_rw = [rw_state[h] for h, _ in heads]
    s_ml = [ml_state[h] for h, _ in heads]
    m_ml = [ml_m[h:h + 1, 0:1] for h, _ in heads]
    s_rt = [rt_state[h] for h, _ in heads]
    for c in range(n_chunks):
        sg = [_dot(s_rw[h], rw_g[c, h]) for h, _ in heads]
        for h, hs in heads:
            inc = rw_h[c, h]
            rw_h[c, h] = s_rw[h]
            s_rw[h] = s_rw[h] * rw_dec[c, 0:1, hs] - sg[h] + inc

            m_chunk = ml_sc[c, h:h + 1, 0:1]
            b_last = ml_sc[c, N_HEADS + h:N_HEADS + h + 1, 0:1]
            m_new = jnp.maximum(b_last + m_ml[h], m_chunk)
            s_old = jnp.exp(b_last + m_ml[h] - m_new)
            s_new = jnp.exp(m_chunk - m_new)
            inc = ml_c[c, h]
            ml_c[c, h] = s_ml[h]
            ml_sc[c, h:h + 1, :] = jnp.broadcast_to(m_ml[h], (1, 128))
            s_ml[h] = s_old * s_ml[h] + s_new * inc
            m_ml[h] = m_new

            inc = rt_r[c, h]
            rt_r[c, h] = s_rt[h]
            s_rt[h] = g_chunk[:, hs] * s_rt[h] + inc
    for h, _ in heads:
        rw_state[h] = s_rw[h]
        ml_state[h] = s_ml[h]
        ml_m[h:h + 1, :] = jnp.broadcast_to(m_ml[h], (1, 128))
        rt_state[h] = s_rt[h]

    def pass3(i, carry):
        cs = [i * PASS3_CHUNKS + j for j in range(PASS3_CHUNKS)]
        chains = [(j, h, hs) for j in range(PASS3_CHUNKS) for h, hs in heads]
        rows = [rows_of(c) for c in cs]
        ld = lambda ref: [ref[r, :] for r in rows]
        rw_c, q_c, k_c2, v_c2, q_c3, qx_c, k_c3, v_c3 = (
            ld(ref) for ref in (rw_rw, ml_q, ml_k, ml_v, rt_q, rt_qx, rt_k, rt_v))
        y_rw = [_nt(rw_c[j][:, hs], rw_h[cs[j], h]) for j, h, hs in chains]
        qk_ml = [_nt(q_c[j][:, hs], k_c2[j][:, hs]) for j, h, hs in chains]
        qk_rt = [_nt(q_c3[j][:, hs], k_c3[j][:, hs]) for j, h, hs in chains]
        in_ml = [_dot(q_c[j][:, hs], ml_c[cs[j], h]) for j, h, hs in chains]
        in_rt = [_dot(qx_c[j][:, hs], rt_r[cs[j], h]) for j, h, hs in chains]

        gb = ld(ml_gb)
        gb_t = [ml_gbt[c] for c in cs]
        sc_ml, m_ts, s_inters = [], [], []
        for n, (j, h, hs) in enumerate(chains):
            b_col = gb[j][:, N_HEADS + h:N_HEADS + h + 1]
            b_row = gb_t[j][N_HEADS + h:N_HEADS + h + 1, :]
            li_row = gb_t[j][h:h + 1, :]
            d_log = jnp.where(tri_i, b_col - b_row + li_row, -jnp.inf)
            m_intra = jnp.max(d_log, axis=-1, keepdims=True)
            m_inter = b_col + ml_sc[cs[j], h:h + 1, 0:1]
            m_t = jnp.maximum(m_inter, m_intra)
            m_ts.append(m_t)
            s_inters.append(jnp.exp(m_inter - m_t))
            sc_ml.append(qk_ml[n] * jnp.exp(d_log - m_t))
        sc_rt = [qk_rt[n] * dec[:, hs] for n, (j, h, hs) in enumerate(chains)]
        tot_ml = [_dot(sc_ml[n], jnp.concatenate([v_c2[j][:, hs], ones_h], axis=1))
                  for n, (j, h, hs) in enumerate(chains)]
        intra_rt = [_dot(sc_rt[n], v_c3[j][:, hs]) for n, (j, h, hs) in enumerate(chains)]
        ys = []
        for n in range(len(chains)):
            tot = tot_ml[n] + s_inters[n] * in_ml[n]
            den = tot[:, HEAD_DIM:2 * HEAD_DIM]
            ys.append(tot[:, 0:HEAD_DIM] / jnp.maximum(jnp.abs(den), jnp.exp(-m_ts[n])))
        for j in range(PASS3_CHUNKS):
            mine = [n for n, ch in enumerate(chains) if ch[0] == j]
            rw_y[rows[j], :] = rw_y[rows[j], :] + jnp.concatenate([y_rw[n] for n in mine], axis=1)
            ml_y[rows[j], :] = jnp.concatenate([ys[n] for n in mine], axis=1)
            rt_y[rows[j], :] = jnp.concatenate([intra_rt[n] + in_rt[n] for n in mine], axis=1)
        return carry

    lax.fori_loop(0, n_chunks // PASS3_CHUNKS, pass3, 0)

    y_a = (_head_norm(rw_y[...], seg, RWKV_GN_EPS) * rw_ln_w + rw_ln_b + bonus)
    mixed[:, 0:GROUP] = y_a * _silu(proj(C_RWG, GROUP))

    u = proj(C_PLU, GROUP)
    ubuf[POOL_HIST:POOL_HIST + T, :] = u
    pos = (t_idx * T + lax.broadcasted_iota(jnp.int32, (T, GROUP), 0) + 1).astype(F32)
    pgrp = lax.broadcasted_iota(jnp.int32, (T, GROUP), 1) // (GROUP // len(POOL_WINDOWS))
    acc = u
    mean = jnp.zeros((T, GROUP), F32)
    j = 1
    for gi, win in enumerate(POOL_WINDOWS):
        while j < win:
            acc = acc + ubuf[POOL_HIST - j:POOL_HIST - j + T, :]
            j += 1
        mean = jnp.where(pgrp == gi, acc / jnp.minimum(pos, float(win)), mean)
    ubuf[0:POOL_HIST, :] = ubuf[T:T + POOL_HIST, :]
    y_b = _dot(mean - u, plw_ref[...]) * pl_scale
    mixed[:, GROUP:2 * GROUP] = y_b * _silu(proj(C_PLG, GROUP))

    hcm = ml_y[...] * _sigmoid(proj(C_MLO, GROUP))
    y_c = _head_norm(hcm, seg, HEAD_NORM_EPS) * mlln_ref[...]
    mixed[:, 2 * GROUP:3 * GROUP] = y_c * _silu(proj(C_MLG, GROUP))

    y_d = _head_norm(rt_y[...], seg, HEAD_NORM_EPS)
    mixed[:, 3 * GROUP:4 * GROUP] = y_d * _silu(proj(C_RTG, GROUP))

    ple = _sigmoid(jnp.dot(xb, ple_gate_ref[...], preferred_element_type=F32)) * _dot(p_ref[...], ple_w_ref[...])
    hres = alpha * x + _dot(mixed[...], w_out_ref[...]) + ple
    lnp = lnp_ref[...]
    mu_ln = jnp.mean(hres, axis=-1, keepdims=True)
    hc = hres - mu_ln
    var = jnp.mean(hc * hc, axis=-1, keepdims=True)
    o_ref[...] = hc * lax.rsqrt(var + LN_EPS) * lnp[0:1, :] + lnp[1:2, :]


def _full(shape):
    return pl.BlockSpec(shape, lambda b, t: (0,) * len(shape))


def _layer_call(x, p, consts, *, alpha, tile_t):
    B, S, _ = x.shape
    T = tile_t
    grid = (B, S // T)
    tile = lambda width: pl.BlockSpec((None, T, width), lambda b, t: (b, t, 0))
    in_specs = [tile(D_MODEL), tile(D_PLE)]
    names = ["w_in", "w_out", "ple_gate", "ple_w", "mu", "rwp", "lora", "plw", "conv", "gbias",
             "mlln", "lnp"]
    in_specs += [_full(consts[n].shape) for n in names]
    in_specs += [pl.BlockSpec((T, 2 * HEAD_DIM), lambda b, t: (t, 0))] * 2
    tail = ["dec", "rtp", "seg", "ltri"]
    in_specs += [_full(consts[n].shape) for n in tail]
    act = lambda width: pltpu.VMEM((T, width), F32)
    n_chunks = T // CHUNK
    per_chunk = lambda r, w: pltpu.VMEM((n_chunks, N_HEADS, r, w), F32)
    scratch = [
        pltpu.VMEM((T + HIST, RWKV_SHIFT), F32),
        pltpu.VMEM((T + HIST, 2 * GROUP), F32),
        pltpu.VMEM((T + POOL_HIST, GROUP), F32),
        act(GROUP), act(GROUP), act(GROUP), act(GROUP), act(GROUP), act(GROUP), act(GROUP),
        pltpu.VMEM((N_HEADS, HEAD_DIM, HEAD_DIM), F32),
        act(GROUP),
        per_chunk(HEAD_DIM, HEAD_DIM), per_chunk(HEAD_DIM, HEAD_DIM),
        pltpu.VMEM((n_chunks, 8, GROUP), F32),
        act(GROUP), act(GROUP), act(GROUP), act(GATE_PAD), act(GROUP),
        pltpu.VMEM((N_HEADS, HEAD_DIM, 2 * HEAD_DIM), F32),
        pltpu.VMEM((8, 128), F32),
        per_chunk(HEAD_DIM, 2 * HEAD_DIM),
        act(GATE_PAD),
        pltpu.VMEM((n_chunks, 8, CHUNK), F32),
        pltpu.VMEM((n_chunks, 8, 128), F32),
        act(GROUP), act(GROUP), act(GROUP), act(GROUP), act(GROUP), act(GROUP),
        pltpu.VMEM((N_HEADS, HEAD_DIM, HEAD_DIM), F32),
        per_chunk(HEAD_DIM, HEAD_DIM),
        act(D_MODEL),
    ]
    args = [x, p] + [consts[n] for n in names] + [consts["cos"], consts["sin"]] + [consts[n] for n in tail]
    return pl.pallas_call(
        functools.partial(_layer_kernel, alpha=alpha),
        grid=grid,
        in_specs=in_specs,
        out_specs=tile(D_MODEL),
        out_shape=jax.ShapeDtypeStruct((B, S, D_MODEL), F32),
        scratch_shapes=scratch,
        compiler_params=pltpu.CompilerParams(
            dimension_semantics=("arbitrary", "arbitrary"),
            vmem_limit_bytes=56 * 1024 * 1024),
        name="deepnorm_mixer_layer",
    )(*args)


def _pack_w_in(w_in):
    g0 = 2432
    body = jnp.concatenate([w_in[:, :g0], w_in[:, g0 + 2 * N_HEADS:]], axis=1)
    perm = np.arange(body.shape[1])
    deint = np.concatenate([np.arange(0, HEAD_DIM, 2), np.arange(1, HEAD_DIM, 2)])
    for c0 in (C_RTQ, C_RTK):
        for h in range(N_HEADS):
            base = c0 + h * HEAD_DIM
            perm[base:base + HEAD_DIM] = base + deint
    body = body[:, perm]
    gates = jnp.pad(w_in[:, g0:g0 + 2 * N_HEADS], ((0, 0), (0, GATE_PAD - 2 * N_HEADS)))
    return jnp.concatenate([body, gates], axis=1).astype(BF16)


def _block_diag(blocks):
    n = len(blocks)
    rows = []
    for i, blk in enumerate(blocks):
        z = jnp.zeros_like(blk)
        rows.append(jnp.concatenate([blk if j == i else z for j in range(n)], axis=1))
    return jnp.concatenate(rows, axis=0)


def kernel(x, p, w_in, rw_mu, rw_w0, rw_w2, rw_a0, rw_a2, rw_kk, rw_ka, rw_rk, rw_ln_w, rw_ln_b, pl_w, pl_scale, ml_conv, ml_ib, ml_fb, ml_ln_w, w_out, ple_w, ple_gate, ln_g, ln_b):
    B, S, D = x.shape
    depth = w_in.shape[0]
    alpha = (2.0 * depth) ** 0.25
    tile_t = min(TILE_T, S)
    assert D == D_MODEL and S % tile_t == 0 and tile_t % CHUNK == 0

    pos = jnp.arange(S, dtype=F32)
    inv_freq = ROPE_BASE ** (-jnp.arange(0, HEAD_DIM, 2, dtype=F32) / HEAD_DIM)
    ang = pos[:, None] * inv_freq[None, :]
    cos, sin = jnp.cos(ang), jnp.sin(ang)
    cos_t = jnp.concatenate([cos, cos, cos, cos], axis=1)
    sin_t = jnp.concatenate([-sin, sin, -sin, sin], axis=1)
    log_g = jnp.log1p(-jnp.exp2(-5.0 - jnp.arange(N_HEADS, dtype=F32)))
    cpos = jnp.arange(CHUNK, dtype=F32)
    rel = cpos[:, None] - cpos[None, :]
    dec = jnp.where(rel >= 0, jnp.exp(log_g[:, None, None] * jnp.maximum(rel, 0.0)), 0.0)
    dec = dec.transpose(1, 0, 2).reshape(CHUNK, GROUP)
    zeta = jnp.exp(log_g[:, None] * (CHUNK - 1.0 - cpos))
    xi = jnp.exp(log_g[:, None] * (cpos + 1.0))
    g_chunk = jnp.exp(log_g * CHUNK)
    lanes = lambda t: jnp.repeat(t.T, HEAD_DIM, axis=1)
    rtp = jnp.concatenate([lanes(xi), lanes(zeta),
                           jnp.broadcast_to(jnp.repeat(g_chunk, HEAD_DIM)[None, :], (CHUNK, GROUP))], axis=1)
    head_of = np.arange(GROUP) // HEAD_DIM
    seg = jnp.asarray(head_of[:, None] == head_of[None, :], BF16)
    ltri = jnp.asarray(np.tril(np.ones((CHUNK, CHUNK))), BF16)

    for i in range(depth):
        row = lambda t: t.reshape(1, -1).astype(F32)
        consts = {
            "w_in": _pack_w_in(w_in[i]),
            "w_out": w_out[i].astype(BF16),
            "ple_gate": ple_gate[i].astype(BF16),
            "ple_w": ple_w[i].astype(BF16),
            "mu": row(rw_mu[i]),
            "rwp": jnp.stack([rw_w0[i], rw_a0[i], rw_kk[i], rw_ka[i], rw_rk[i], rw_ln_w[i],
                              rw_ln_b[i], pl_scale[i]]).astype(F32),
            "lora": _block_diag([rw_w2[i], rw_a2[i]]).astype(BF16),
            "plw": _block_diag([pl_w[i, g] for g in range(len(POOL_WINDOWS))]).astype(BF16),
            "conv": jnp.pad(ml_conv[i].astype(F32), ((0, 8 - CONV_K), (0, 0))),
            "gbias": jnp.pad(jnp.concatenate([ml_ib[i], ml_fb[i]]).astype(F32),
                             (0, GATE_PAD - 2 * N_HEADS)).reshape(1, GATE_PAD),
            "mlln": row(ml_ln_w[i]),
            "lnp": jnp.stack([ln_g[i], ln_b[i]]).astype(F32),
            "cos": cos_t, "sin": sin_t, "dec": dec, "rtp": rtp, "seg": seg, "ltri": ltri,
        }
        x = _layer_call(x, p[i], consts, alpha=alpha, tile_t=tile_t)
    return x
```

```python
import functools

import jax
import jax.numpy as jnp
import numpy as np
from jax import lax
from jax.experimental import pallas as pl
from jax.experimental.pallas import tpu as pltpu

D_MODEL = 1024
N_HEADS = 4
HEAD_DIM = 64
GROUP = 256
CHUNK = 64
PAIR_W = 2 * HEAD_DIM
N_PAIRS = GROUP // PAIR_W
LORA = 64
POOL_WINDOWS = (2, 4, 8, 16)
POOL_HIST = 16
CONV_K = 4
HIST = 8
D_PLE = 256
ROPE_BASE = 10000.0
LN_EPS = 1e-5
HEAD_NORM_EPS = 1e-6
RWKV_GN_EPS = 64e-5
RWKV_SHIFT = 3 * GROUP + 2 * LORA
GATE_PAD = 128

C_RW = 0
C_RWG = C_RW + RWKV_SHIFT
C_PLU = C_RWG + GROUP
C_PLG = C_PLU + GROUP
C_MLQK = C_PLG + GROUP
C_MLV = C_MLQK + 2 * GROUP
C_MLO = C_MLV + GROUP
C_MLG = C_MLO + GROUP
C_RTQ = C_MLG + GROUP
C_RTK = C_RTQ + GROUP
C_RTV = C_RTK + GROUP
C_RTG = C_RTV + GROUP
C_GATE = C_RTG + GROUP
N_COLS_PACKED = C_GATE + GATE_PAD

TILE_T = 512
PASS1_CHUNKS = 8
PASS3_CHUNKS = 2

F32 = jnp.float32
BF16 = jnp.bfloat16


def _dot(a, b):
    return jnp.dot(a.astype(BF16), b.astype(BF16), preferred_element_type=F32)


def _nt(a, b):
    return lax.dot_general(a.astype(BF16), b.astype(BF16), (((1,), (1,)), ((), ())),
                           preferred_element_type=F32)


def _tn(a, b):
    return lax.dot_general(a.astype(BF16), b.astype(BF16), (((0,), (0,)), ((), ())),
                           preferred_element_type=F32)


def _split3(x):
    hi = x.astype(BF16)
    r1 = x - hi.astype(F32)
    mid = r1.astype(BF16)
    lo = (r1 - mid.astype(F32)).astype(BF16)
    return hi, mid, lo


def _dot_exact_lhs(m, x):
    hi, mid, lo = _split3(x)
    d = lambda t: jnp.dot(m, t, preferred_element_type=F32)
    return d(hi) + d(mid) + d(lo)


def _nt_exact_lhs(m, x):
    hi, mid, lo = _split3(x)
    d = lambda t: lax.dot_general(m, t, (((1,), (1,)), ((), ())), preferred_element_type=F32)
    return d(hi) + d(mid) + d(lo)


def _segsum(x, seg):
    hi = x.astype(BF16)
    lo = (x - hi.astype(F32)).astype(BF16)
    return (jnp.dot(hi, seg, preferred_element_type=F32)
            + jnp.dot(lo, seg, preferred_element_type=F32))


def _head_norm(y, seg, eps):
    mu = _segsum(y, seg) * (1.0 / HEAD_DIM)
    yc = y - mu
    var = _segsum(yc * yc, seg) * (1.0 / HEAD_DIM)
    return yc * lax.rsqrt(var + eps)


def _softplus(x):
    return jnp.maximum(x, 0.0) + jnp.log(1.0 + jnp.exp(-jnp.abs(x)))


def _sigmoid(x):
    return 1.0 / (1.0 + jnp.exp(-x))


def _silu(x):
    return x * _sigmoid(x)


def _layer_kernel(
        x_ref, p_ref, w_in_ref, w_out_ref, ple_gate_ref, ple_w_ref,
        mu_ref, rwp_ref, lora_ref, plw_ref, conv_ref, gbias_ref, mlln_ref, lnp_ref,
        cos_ref, sin_ref, dec_ref, rtp_ref, seg_ref, ltri_ref,
        o_ref,
        zbuf, cbuf, ubuf,
        rw_r, rw_k, rw_v, rw_lw, rw_kk, rw_b, rw_y, rw_state,
        rw_rw, rw_g, rw_h, rw_dec,
        ml_q, ml_k, ml_v, ml_gt, ml_y, ml_state, ml_m,
        ml_c, ml_gb, ml_gbt, ml_sc,
        rt_q, rt_qx, rt_k, rt_kz, rt_v, rt_y, rt_state,
        rt_r,
        mixed,
        *, alpha):
    T = x_ref.shape[0]
    n_chunks = T // CHUNK
    t_idx = pl.program_id(1)

    @pl.when(t_idx == 0)
    def _reset():
        zbuf[0:HIST, :] = jnp.zeros((HIST, RWKV_SHIFT), F32)
        cbuf[0:HIST, :] = jnp.zeros((HIST, 2 * GROUP), F32)
        ubuf[0:POOL_HIST, :] = jnp.zeros((POOL_HIST, GROUP), F32)
        rw_state[...] = jnp.zeros_like(rw_state)
        ml_state[...] = jnp.zeros_like(ml_state)
        ml_m[...] = jnp.zeros_like(ml_m)
        rt_state[...] = jnp.zeros_like(rt_state)

    x = x_ref[...]
    xb = x.astype(BF16)
    seg = seg_ref[...]
    ltri = ltri_ref[...]

    def proj(c0, width):
        return jnp.dot(xb, w_in_ref[:, c0:c0 + width], preferred_element_type=F32)

    zrw = proj(C_RW, RWKV_SHIFT)
    zbuf[HIST:HIST + T, :] = zrw
    prev = zbuf[HIST - 1:HIST - 1 + T, :]
    zs = zrw + (prev - zrw) * mu_ref[...]
    zbuf[0:HIST, :] = zbuf[T:T + HIST, :]
    r = zs[:, 0:GROUP]
    k = zs[:, GROUP:2 * GROUP]
    v = zs[:, 2 * GROUP:3 * GROUP]
    wa = zs[:, 3 * GROUP:RWKV_SHIFT]
    lane = lax.broadcasted_iota(jnp.int32, wa.shape, 1)
    wa = jnp.where(lane < LORA, jnp.tanh(wa), wa)
    lo = _dot(wa, lora_ref[...])
    rwp = rwp_ref[...]
    w0, a0, k_k, k_a, r_k, rw_ln_w, rw_ln_b, pl_scale = (rwp[i:i + 1, :] for i in range(8))
    w_log = -_softplus(-(w0 + lo[:, 0:GROUP])) - 0.5
    a = _sigmoid(a0 + lo[:, GROUP:2 * GROUP])
    kk = k * k_k
    kk = kk / jnp.maximum(jnp.sqrt(_segsum(kk * kk, seg)), 1e-12)
    k2 = k * (1.0 + (a - 1.0) * k_a)
    rw_r[...] = r
    rw_k[...] = k2
    rw_v[...] = v
    rw_lw[...] = -jnp.exp(w_log)
    rw_kk[...] = kk
    rw_b[...] = kk * a
    bonus = _segsum(r * k2 * r_k, seg) * v

    zqk = proj(C_MLQK, 2 * GROUP)
    cbuf[HIST:HIST + T, :] = zqk
    cw = conv_ref[...]
    conv = zqk * cw[CONV_K - 1:CONV_K, :]
    for j in range(1, CONV_K):
        conv = conv + cbuf[HIST - j:HIST - j + T, :] * cw[CONV_K - 1 - j:CONV_K - j, :]
    cbuf[0:HIST, :] = cbuf[T:T + HIST, :]
    qk = _silu(conv)
    ml_q[...] = qk[:, 0:GROUP]
    ml_k[...] = qk[:, GROUP:2 * GROUP] * (HEAD_DIM ** -0.5)
    ml_v[...] = proj(C_MLV, GROUP)
    zg = proj(C_GATE, GATE_PAD) + gbias_ref[...]
    glane = lax.broadcasted_iota(jnp.int32, zg.shape, 1)
    ml_gt[...] = jnp.where(glane < N_HEADS, zg,
                           jnp.where(glane < 2 * N_HEADS, -_softplus(-zg), 0.0))

    cos = cos_ref[...]
    sin = sin_ref[...]
    cos2 = jnp.concatenate([cos, cos], axis=1)
    sin2 = jnp.concatenate([sin, sin], axis=1)
    hl = lax.broadcasted_iota(jnp.int32, (T, GROUP), 1) % HEAD_DIM

    def rotary(z):
        swapped = jnp.where(hl < HEAD_DIM // 2,
                            pltpu.roll(z, GROUP - HEAD_DIM // 2, axis=1),
                            pltpu.roll(z, HEAD_DIM // 2, axis=1))
        return z * cos2 + swapped * sin2

    rtp = rtp_ref[...]
    q_rot = rotary(proj(C_RTQ, GROUP))
    k_rot = rotary(proj(C_RTK, GROUP)) * (HEAD_DIM ** -0.5)
    rt_q[...] = q_rot
    rt_k[...] = k_rot
    rt_v[...] = proj(C_RTV, GROUP)
    for c in range(n_chunks):
        rows = slice(c * CHUNK, (c + 1) * CHUNK)
        rt_qx[rows, :] = q_rot[rows, :] * rtp[:, 0:GROUP]
        rt_kz[rows, :] = k_rot[rows, :] * rtp[:, GROUP:2 * GROUP]

    PAIR = PAIR_W
    ri = lax.broadcasted_iota(jnp.int32, (PAIR, PAIR), 0)
    ci = lax.broadcasted_iota(jnp.int32, (PAIR, PAIR), 1)
    same_blk = (ri // HEAD_DIM) == (ci // HEAD_DIM)
    ti = lax.broadcasted_iota(jnp.int32, (CHUNK, PAIR), 0)
    si = lax.broadcasted_iota(jnp.int32, (CHUNK, PAIR), 1) % HEAD_DIM
    tri_s = ti > si
    tri_i = ti >= si
    eye = jnp.where(ti == si, 1.0, 0.0).astype(F32)
    first = lax.broadcasted_iota(jnp.int32, (CHUNK, PAIR), 1) < HEAD_DIM
    first_row = lax.broadcasted_iota(jnp.int32, (1, 2 * PAIR), 1) % PAIR < HEAD_DIM
    e8 = jnp.where(lax.broadcasted_iota(jnp.int32, (8, GATE_PAD), 0)
                   == lax.broadcasted_iota(jnp.int32, (8, GATE_PAD), 1), 1.0, 0.0).astype(BF16)
    ones_bd = jnp.where(same_blk, 1.0, 0.0).astype(BF16)
    dec = dec_ref[...]
    g_chunk = rtp[0:1, 2 * GROUP:3 * GROUP]
    pairs = [(p, slice(p * PAIR, (p + 1) * PAIR)) for p in range(N_PAIRS)]

    def rows_of(c):
        return pl.ds(pl.multiple_of(c * CHUNK, CHUNK), CHUNK)

    def dup(x):
        return jnp.concatenate([x, x], axis=0)

    def stack(x):
        return jnp.where(same_blk, dup(x), 0.0)

    def stack2(x):
        return jnp.concatenate([stack(x[:, 0:PAIR]), stack(x[:, PAIR:2 * PAIR])], axis=1)


    def pass1(i, carry):
        cs = [i * PASS1_CHUNKS + j for j in range(PASS1_CHUNKS)]
        chains = [(j, p, ps) for j in range(PASS1_CHUNKS) for p, ps in pairs]

        prep = []
        for c in cs:
            rows = rows_of(c)
            lw = rw_lw[rows, :]
            cum = _dot_exact_lhs(ltri, lw)
            g_l = cum[CHUNK - 1:CHUNK, :]
            e_neg = jnp.exp(-cum)
            kk_c = rw_kk[rows, :]
            b_c = rw_b[rows, :]
            k_c = rw_k[rows, :]
            e_g = jnp.exp(g_l - cum)
            rw_dec[c] = jnp.broadcast_to(jnp.exp(g_l), (8, GROUP))
            prep.append(dict(al=jnp.exp(cum - lw) * kk_c, be=b_c * e_neg, kt=k_c * e_neg,
                             rt=rw_r[rows, :] * jnp.exp(cum), kh=k_c * e_g, bh=b_c * e_g,
                             v=rw_v[rows, :]))
        pick = lambda name: [prep[j][name][:, ps] for j, p, ps in chains]
        al, be, kt, rt, kh, bh, vv = (pick(n) for n in ("al", "be", "kt", "rt", "kh", "bh", "v"))
        ar = [jnp.concatenate([a, r], axis=0) for a, r in zip(al, rt)]
        nr = [_nt(x, stack(y)) for x, y in zip(ar, be)]
        pq = [_nt(x, stack(y)) for x, y in zip(ar, kt)]
        n_m = [jnp.where(tri_s, t[0:CHUNK], 0.0) for t in nr]
        rb_m = [jnp.where(tri_i, t[CHUNK:2 * CHUNK], 0.0) for t in nr]
        p_m = [jnp.where(tri_s, t[0:CHUNK], 0.0) for t in pq]
        q_m = [jnp.where(tri_i, t[CHUNK:2 * CHUNK], 0.0) for t in pq]
        pw = [_dot(n, stack(n)) for n in n_m]
        pqv = [_dot(jnp.concatenate([x, y], axis=0), stack(v)) for x, y, v in zip(p_m, q_m, vv)]
        tinv = [eye - n for n in n_m]

        late = []
        for c in cs:
            rows = rows_of(c)
            gt = ml_gt[rows, :]
            bcum = _dot_exact_lhs(ltri, gt)
            gl2 = lax.broadcasted_iota(jnp.int32, gt.shape, 1)
            gb = jnp.where(gl2 < N_HEADS, gt, bcum)
            gb_t = _nt_exact_lhs(e8, gb)
            ml_gb[rows, :] = gb
            ml_gbt[c] = gb_t
            b_last4 = gb_t[N_HEADS:2 * N_HEADS, CHUNK - 1:CHUNK]
            m_chunk4 = jnp.max(b_last4 - gb_t[N_HEADS:2 * N_HEADS, :] + gb_t[0:N_HEADS, :],
                               axis=-1, keepdims=True)
            ml_sc[c] = jnp.concatenate([jnp.broadcast_to(m_chunk4, (N_HEADS, 128)),
                                        jnp.broadcast_to(b_last4, (N_HEADS, 128))], axis=0)
            k_c2 = ml_k[rows, :]
            v_c2 = ml_v[rows, :]
            kz_c = rt_kz[rows, :]
            v_c3 = rt_v[rows, :]
            for p, ps in pairs:
                wg = []
                for h in (2 * p, 2 * p + 1):
                    b_col = gb[:, N_HEADS + h:N_HEADS + h + 1]
                    li_col = gb[:, h:h + 1]
                    wg.append(jnp.exp(b_last4[h:h + 1] - b_col + li_col - m_chunk4[h:h + 1]))
                wgt = jnp.where(first, wg[0], wg[1])
                late.append((c, p, k_c2[:, ps], jnp.concatenate([v_c2[:, ps] * wgt, wgt], axis=1),
                             kz_c[:, ps], v_c3[:, ps]))

        for _ in range(4):
            pw_s = [stack(x).astype(BF16) for x in pw]
            pw_next = [_dot(x, y) for x, y in zip(pw, pw_s)]
            tinv = [t + _dot(t, y) for t, y in zip(tinv, pw_s)]
            pw = pw_next
        tinv = [t + _dot(t, stack(x)) for t, x in zip(tinv, pw)]
        wu = [_dot(t, jnp.concatenate([stack(a), stack(x[0:CHUNK])], axis=1))
              for t, a, x in zip(tinv, al, pqv)]
        rbwu = [_dot(x, stack2(y)) for x, y in zip(rb_m, wu)]
        gu = [_tn(y, x) for y, x in zip(wu, bh)]
        vk = [_tn(x, y) for x, y in zip(vv, kh)]
        for c, p, k_p, vw, kz_p, v_p in late:
            cc = _tn(k_p, vw)
            ml_c[c, p] = jnp.concatenate([jnp.where(same_blk, cc[:, 0:PAIR], 0.0),
                                          jnp.where(same_blk, cc[:, PAIR:2 * PAIR], 0.0)], axis=1)
            rt_r[c, p] = jnp.where(same_blk, _tn(kz_p, v_p), 0.0)
        for n, (j, p, ps) in enumerate(chains):
            rw_g[cs[j], p] = jnp.where(same_blk, gu[n][0:PAIR], 0.0)
            rw_h[cs[j], p] = jnp.where(same_blk, vk[n] - gu[n][PAIR:2 * PAIR], 0.0)
            rows = rows_of(cs[j])
            rw_rw[rows, ps] = rt[n] - rbwu[n][:, 0:PAIR]
            rw_y[rows, ps] = pqv[n][CHUNK:2 * CHUNK] - rbwu[n][:, PAIR:2 * PAIR]
        return carry

    lax.fori_loop(0, n_chunks // PASS1_CHUNKS, pass1, 0)

    s_rw = [rw_state[p] for p, _ in pairs]
    s_ml = [ml_state[p] for p, _ in pairs]
    m_ml = [ml_m[h:h + 1, 0:1] for h in range(N_HEADS)]
    s_rt = [rt_state[p] for p, _ in pairs]
    for c in range(n_chunks):
        sg = [_dot(s_rw[p], rw_g[c, p]) for p, _ in pairs]
        for p, ps in pairs:
            inc = rw_h[c, p]
            rw_h[c, p] = s_rw[p]
            s_rw[p] = s_rw[p] * rw_dec[c, 0:1, ps] - sg[p] + inc

            s_old, s_new = [], []
            for h in (2 * p, 2 * p + 1):
                m_chunk = ml_sc[c, h:h + 1, 0:1]
                b_last = ml_sc[c, N_HEADS + h:N_HEADS + h + 1, 0:1]
                m_new = jnp.maximum(b_last + m_ml[h], m_chunk)
                s_old.append(jnp.exp(b_last + m_ml[h] - m_new))
                s_new.append(jnp.exp(m_chunk - m_new))
                ml_sc[c, h:h + 1, :] = jnp.broadcast_to(m_ml[h], (1, 128))
                m_ml[h] = m_new
            inc = ml_c[c, p]
            ml_c[c, p] = s_ml[p]
            s_ml[p] = (jnp.where(first_row, s_old[0], s_old[1]) * s_ml[p]
                       + jnp.where(first_row, s_new[0], s_new[1]) * inc)

            inc = rt_r[c, p]
            rt_r[c, p] = s_rt[p]
            s_rt[p] = g_chunk[:, ps] * s_rt[p] + inc
    for p, _ in pairs:
        rw_state[p] = s_rw[p]
        ml_state[p] = s_ml[p]
        rt_state[p] = s_rt[p]
    for h in range(N_HEADS):
        ml_m[h:h + 1, :] = jnp.broadcast_to(m_ml[h], (1, 128))

    def pass3(i, carry):
        cs = [i * PASS3_CHUNKS + j for j in range(PASS3_CHUNKS)]
        chains = [(j, p, ps) for j in range(PASS3_CHUNKS) for p, ps in pairs]
        rows = [rows_of(c) for c in cs]
        ld = lambda ref: [ref[rows[j], ps] for j, p, ps in chains]
        rw_c, q_c, k_c2, v_c2, q_c3, qx_c, k_c3, v_c3 = (
            ld(ref) for ref in (rw_rw, ml_q, ml_k, ml_v, rt_q, rt_qx, rt_k, rt_v))
        y_rw = [_nt(rw_c[n], rw_h[cs[j], p]) for n, (j, p, ps) in enumerate(chains)]
        qk_ml = [_nt(q_c[n], stack(k_c2[n])) for n in range(len(chains))]
        qk_rt = [_nt(q_c3[n], stack(k_c3[n])) for n in range(len(chains))]
        in_ml = [_dot(q_c[n], ml_c[cs[j], p]) for n, (j, p, ps) in enumerate(chains)]
        in_rt = [_dot(qx_c[n], rt_r[cs[j], p]) for n, (j, p, ps) in enumerate(chains)]

        gb = [ml_gb[r, :] for r in rows]
        gb_t = [ml_gbt[c] for c in cs]
        sc_ml, floor, s_int = [], [], []
        neg_inf = jnp.full((CHUNK, PAIR), -jnp.inf, F32)
        for n, (j, p, ps) in enumerate(chains):
            h0, h1 = 2 * p, 2 * p + 1
            b_col = jnp.where(first, gb[j][:, N_HEADS + h0:N_HEADS + h0 + 1],
                              gb[j][:, N_HEADS + h1:N_HEADS + h1 + 1])
            b_row = jnp.concatenate([gb_t[j][N_HEADS + h0:N_HEADS + h0 + 1, :],
                                     gb_t[j][N_HEADS + h1:N_HEADS + h1 + 1, :]], axis=1)
            li_row = jnp.concatenate([gb_t[j][h0:h0 + 1, :], gb_t[j][h1:h1 + 1, :]], axis=1)
            d_log = jnp.where(tri_i, b_col - b_row + li_row, neg_inf)
            m_intra = jnp.where(first,
                                jnp.max(jnp.where(first, d_log, neg_inf), axis=-1, keepdims=True),
                                jnp.max(jnp.where(first, neg_inf, d_log), axis=-1, keepdims=True))
            m_inter = b_col + jnp.where(first, ml_sc[cs[j], h0:h0 + 1, 0:1], ml_sc[cs[j], h1:h1 + 1, 0:1])
            m_t = jnp.maximum(m_inter, m_intra)
            floor.append(jnp.exp(-m_t))
            s_int.append(jnp.exp(m_inter - m_t))
            sc_ml.append(qk_ml[n] * jnp.exp(d_log - m_t))
        sc_rt = [qk_rt[n] * dec[:, ps] for n, (j, p, ps) in enumerate(chains)]
        tot_ml = [_dot(sc_ml[n], jnp.concatenate([stack(v_c2[n]).astype(BF16), ones_bd], axis=1))
                  for n in range(len(chains))]
        intra_rt = [_dot(sc_rt[n], stack(v_c3[n])) for n in range(len(chains))]
        for n, (j, p, ps) in enumerate(chains):
            tot = tot_ml[n] + jnp.concatenate([s_int[n], s_int[n]], axis=1) * in_ml[n]
            rw_y[rows[j], ps] = rw_y[rows[j], ps] + y_rw[n]
            ml_y[rows[j], ps] = tot[:, 0:PAIR] / jnp.maximum(jnp.abs(tot[:, PAIR:2 * PAIR]), floor[n])
            rt_y[rows[j], ps] = intra_rt[n] + in_rt[n]
        return carry

    lax.fori_loop(0, n_chunks // PASS3_CHUNKS, pass3, 0)

    y_a = (_head_norm(rw_y[...], seg, RWKV_GN_EPS) * rw_ln_w + rw_ln_b + bonus)
    mixed[:, 0:GROUP] = y_a * _silu(proj(C_RWG, GROUP))

    u = proj(C_PLU, GROUP)
    ubuf[POOL_HIST:POOL_HIST + T, :] = u
    pos = (t_idx * T + lax.broadcasted_iota(jnp.int32, (T, GROUP), 0) + 1).astype(F32)
    pgrp = lax.broadcasted_iota(jnp.int32, (T, GROUP), 1) // (GROUP // len(POOL_WINDOWS))
    acc = u
    mean = jnp.zeros((T, GROUP), F32)
    j = 1
    for gi, win in enumerate(POOL_WINDOWS):
        while j < win:
            acc = acc + ubuf[POOL_HIST - j:POOL_HIST - j + T, :]
            j += 1
        mean = jnp.where(pgrp == gi, acc / jnp.minimum(pos, float(win)), mean)
    ubuf[0:POOL_HIST, :] = ubuf[T:T + POOL_HIST, :]
    y_b = _dot(mean - u, plw_ref[...]) * pl_scale
    mixed[:, GROUP:2 * GROUP] = y_b * _silu(proj(C_PLG, GROUP))

    hcm = ml_y[...] * _sigmoid(proj(C_MLO, GROUP))
    y_c = _head_norm(hcm, seg, HEAD_NORM_EPS) * mlln_ref[...]
    mixed[:, 2 * GROUP:3 * GROUP] = y_c * _silu(proj(C_MLG, GROUP))

    y_d = _head_norm(rt_y[...], seg, HEAD_NORM_EPS)
    mixed[:, 3 * GROUP:4 * GROUP] = y_d * _silu(proj(C_RTG, GROUP))

    ple = _sigmoid(jnp.dot(xb, ple_gate_ref[...], preferred_element_type=F32)) * _dot(p_ref[...], ple_w_ref[...])
    hres = alpha * x + _dot(mixed[...], w_out_ref[...]) + ple
    lnp = lnp_ref[...]
    mu_ln = jnp.mean(hres, axis=-1, keepdims=True)
    hc = hres - mu_ln
    var = jnp.mean(hc * hc, axis=-1, keepdims=True)
    o_ref[...] = hc * lax.rsqrt(var + LN_EPS) * lnp[0:1, :] + lnp[1:2, :]


def _full(shape):
    return pl.BlockSpec(shape, lambda b, t: (0,) * len(shape))


def _layer_call(x, p, consts, *, alpha, tile_t):
    B, S, _ = x.shape
    T = tile_t
    grid = (B, S // T)
    tile = lambda width: pl.BlockSpec((None, T, width), lambda b, t: (b, t, 0))
    in_specs = [tile(D_MODEL), tile(D_PLE)]
    names = ["w_in", "w_out", "ple_gate", "ple_w", "mu", "rwp", "lora", "plw", "conv", "gbias",
             "mlln", "lnp"]
    in_specs += [_full(consts[n].shape) for n in names]
    in_specs += [pl.BlockSpec((T, 2 * HEAD_DIM), lambda b, t: (t, 0))] * 2
    tail = ["dec", "rtp", "seg", "ltri"]
    in_specs += [_full(consts[n].shape) for n in tail]
    act = lambda width: pltpu.VMEM((T, width), F32)
    n_chunks = T // CHUNK
    per_chunk = lambda r, w: pltpu.VMEM((n_chunks, N_PAIRS, r, w), F32)
    scratch = [
        pltpu.VMEM((T + HIST, RWKV_SHIFT), F32),
        pltpu.VMEM((T + HIST, 2 * GROUP), F32),
        pltpu.VMEM((T + POOL_HIST, GROUP), F32),
        act(GROUP), act(GROUP), act(GROUP), act(GROUP), act(GROUP), act(GROUP), act(GROUP),
        pltpu.VMEM((N_PAIRS, PAIR_W, PAIR_W), F32),
        act(GROUP),
        per_chunk(PAIR_W, PAIR_W), per_chunk(PAIR_W, PAIR_W),
        pltpu.VMEM((n_chunks, 8, GROUP), F32),
        act(GROUP), act(GROUP), act(GROUP), act(GATE_PAD), act(GROUP),
        pltpu.VMEM((N_PAIRS, PAIR_W, 2 * PAIR_W), F32),
        pltpu.VMEM((8, 128), F32),
        per_chunk(PAIR_W, 2 * PAIR_W),
        act(GATE_PAD),
        pltpu.VMEM((n_chunks, 8, CHUNK), F32),
        pltpu.VMEM((n_chunks, 8, 128), F32),
        act(GROUP), act(GROUP), act(GROUP), act(GROUP), act(GROUP), act(GROUP),
        pltpu.VMEM((N_PAIRS, PAIR_W, PAIR_W), F32),
        per_chunk(PAIR_W, PAIR_W),
        act(D_MODEL),
    ]
    args = [x, p] + [consts[n] for n in names] + [consts["cos"], consts["sin"]] + [consts[n] for n in tail]
    return pl.pallas_call(
        functools.partial(_layer_kernel, alpha=alpha),
        grid=grid,
        in_specs=in_specs,
        out_specs=tile(D_MODEL),
        out_shape=jax.ShapeDtypeStruct((B, S, D_MODEL), F32),
        scratch_shapes=scratch,
        compiler_params=pltpu.CompilerParams(
            dimension_semantics=("arbitrary", "arbitrary"),
            vmem_limit_bytes=56 * 1024 * 1024),
        name="deepnorm_mixer_layer",
    )(*args)


def _pack_w_in(w_in):
    g0 = 2432
    body = jnp.concatenate([w_in[:, :g0], w_in[:, g0 + 2 * N_HEADS:]], axis=1)
    perm = np.arange(body.shape[1])
    deint = np.concatenate([np.arange(0, HEAD_DIM, 2), np.arange(1, HEAD_DIM, 2)])
    for c0 in (C_RTQ, C_RTK):
        for h in range(N_HEADS):
            base = c0 + h * HEAD_DIM
            perm[base:base + HEAD_DIM] = base + deint
    body = body[:, perm]
    gates = jnp.pad(w_in[:, g0:g0 + 2 * N_HEADS], ((0, 0), (0, GATE_PAD - 2 * N_HEADS)))
    return jnp.concatenate([body, gates], axis=1).astype(BF16)


def _block_diag(blocks):
    n = len(blocks)
    rows = []
    for i, blk in enumerate(blocks):
        z = jnp.zeros_like(blk)
        rows.append(jnp.concatenate([blk if j == i else z for j in range(n)], axis=1))
    return jnp.concatenate(rows, axis=0)


def kernel(x, p, w_in, rw_mu, rw_w0, rw_w2, rw_a0, rw_a2, rw_kk, rw_ka, rw_rk, rw_ln_w, rw_ln_b, pl_w, pl_scale, ml_conv, ml_ib, ml_fb, ml_ln_w, w_out, ple_w, ple_gate, ln_g, ln_b):
    B, S, D = x.shape
    depth = w_in.shape[0]
    alpha = (2.0 * depth) ** 0.25
    tile_t = min(TILE_T, S)
    assert D == D_MODEL and S % tile_t == 0 and tile_t % CHUNK == 0

    pos = jnp.arange(S, dtype=F32)
    inv_freq = ROPE_BASE ** (-jnp.arange(0, HEAD_DIM, 2, dtype=F32) / HEAD_DIM)
    ang = pos[:, None] * inv_freq[None, :]
    cos, sin = jnp.cos(ang), jnp.sin(ang)
    cos_t = jnp.concatenate([cos, cos, cos, cos], axis=1)
    sin_t = jnp.concatenate([-sin, sin, -sin, sin], axis=1)
    log_g = jnp.log1p(-jnp.exp2(-5.0 - jnp.arange(N_HEADS, dtype=F32)))
    cpos = jnp.arange(CHUNK, dtype=F32)
    rel = cpos[:, None] - cpos[None, :]
    dec = jnp.where(rel >= 0, jnp.exp(log_g[:, None, None] * jnp.maximum(rel, 0.0)), 0.0)
    dec = dec.transpose(1, 0, 2).reshape(CHUNK, GROUP)
    zeta = jnp.exp(log_g[:, None] * (CHUNK - 1.0 - cpos))
    xi = jnp.exp(log_g[:, None] * (cpos + 1.0))
    g_chunk = jnp.exp(log_g * CHUNK)
    lanes = lambda t: jnp.repeat(t.T, HEAD_DIM, axis=1)
    rtp = jnp.concatenate([lanes(xi), lanes(zeta),
                           jnp.broadcast_to(jnp.repeat(g_chunk, HEAD_DIM)[None, :], (CHUNK, GROUP))], axis=1)
    head_of = np.arange(GROUP) // HEAD_DIM
    seg = jnp.asarray(head_of[:, None] == head_of[None, :], BF16)
    ltri = jnp.asarray(np.tril(np.ones((CHUNK, CHUNK))), BF16)

    for i in range(depth):
        row = lambda t: t.reshape(1, -1).astype(F32)
        consts = {
            "w_in": _pack_w_in(w_in[i]),
            "w_out": w_out[i].astype(BF16),
            "ple_gate": ple_gate[i].astype(BF16),
            "ple_w": ple_w[i].astype(BF16),
            "mu": row(rw_mu[i]),
            "rwp": jnp.stack([rw_w0[i], rw_a0[i], rw_kk[i], rw_ka[i], rw_rk[i], rw_ln_w[i],
                              rw_ln_b[i], pl_scale[i]]).astype(F32),
            "lora": _block_diag([rw_w2[i], rw_a2[i]]).astype(BF16),
            "plw": _block_diag([pl_w[i, g] for g in range(len(POOL_WINDOWS))]).astype(BF16),
            "conv": jnp.pad(ml_conv[i].astype(F32), ((0, 8 - CONV_K), (0, 0))),
            "gbias": jnp.pad(jnp.concatenate([ml_ib[i], ml_fb[i]]).astype(F32),
                             (0, GATE_PAD - 2 * N_HEADS)).reshape(1, GATE_PAD),
            "mlln": row(ml_ln_w[i]),
            "lnp": jnp.stack([ln_g[i], ln_b[i]]).astype(F32),
            "cos": cos_t, "sin": sin_t, "dec": dec, "rtp": rtp, "seg": seg, "ltri": ltri,
        }
        x = _layer_call(x, p[i], consts, alpha=alpha, tile_t=tile_t)
    return x
```

```python
import functools

import jax
import jax.numpy as jnp
import numpy as np
from jax import lax
from jax.experimental import pallas as pl
from jax.experimental.pallas import tpu as pltpu

D_MODEL = 1024
N_HEADS = 4
HEAD_DIM = 64
GROUP = 256
CHUNK = 64
PAIR_W = 2 * HEAD_DIM
N_PAIRS = GROUP // PAIR_W
LORA = 64
POOL_WINDOWS = (2, 4, 8, 16)
POOL_HIST = 16
CONV_K = 4
HIST = 8
D_PLE = 256
ROPE_BASE = 10000.0
LN_EPS = 1e-5
HEAD_NORM_EPS = 1e-6
RWKV_GN_EPS = 64e-5
RWKV_SHIFT = 3 * GROUP + 2 * LORA
GATE_PAD = 128

C_RW = 0
C_RWG = C_RW + RWKV_SHIFT
C_PLU = C_RWG + GROUP
C_PLG = C_PLU + GROUP
C_MLQK = C_PLG + GROUP
C_MLV = C_MLQK + 2 * GROUP
C_MLO = C_MLV + GROUP
C_MLG = C_MLO + GROUP
C_RTQ = C_MLG + GROUP
C_RTK = C_RTQ + GROUP
C_RTV = C_RTK + GROUP
C_RTG = C_RTV + GROUP
C_GATE = C_RTG + GROUP
N_COLS_PACKED = C_GATE + GATE_PAD

TILE_T = 512
PASS1_CHUNKS = 8
OUT_ROWS = 256

F32 = jnp.float32
BF16 = jnp.bfloat16


def _dot(a, b):
    return jnp.dot(a.astype(BF16), b.astype(BF16), preferred_element_type=F32)


def _nt(a, b):
    return lax.dot_general(a.astype(BF16), b.astype(BF16), (((1,), (1,)), ((), ())),
                           preferred_element_type=F32)


def _tn(a, b):
    return lax.dot_general(a.astype(BF16), b.astype(BF16), (((0,), (0,)), ((), ())),
                           preferred_element_type=F32)


def _split3(x):
    hi = x.astype(BF16)
    r1 = x - hi.astype(F32)
    mid = r1.astype(BF16)
    lo = (r1 - mid.astype(F32)).astype(BF16)
    return hi, mid, lo


def _dot_exact_lhs(m, x):
    hi, mid, lo = _split3(x)
    d = lambda t: jnp.dot(m, t, preferred_element_type=F32)
    return d(hi) + d(mid) + d(lo)


def _nt_exact_lhs(m, x):
    hi, mid, lo = _split3(x)
    d = lambda t: lax.dot_general(m, t, (((1,), (1,)), ((), ())), preferred_element_type=F32)
    return d(hi) + d(mid) + d(lo)


def _segsum(x, seg):
    return jnp.dot(x.astype(BF16), seg, preferred_element_type=F32)


def _head_norm(y, seg, eps):
    mu = _segsum(y, seg) * (1.0 / HEAD_DIM)
    yc = y - mu
    var = _segsum(yc * yc, seg) * (1.0 / HEAD_DIM)
    return yc * lax.rsqrt(var + eps)


def _softplus(x):
    return jnp.maximum(x, 0.0) + jnp.log(1.0 + jnp.exp(-jnp.abs(x)))


def _sigmoid(x):
    return 0.5 * jnp.tanh(0.5 * x) + 0.5


def _silu(x):
    return x * _sigmoid(x)


def _layer_kernel(
        x_ref, p_ref, wa_ref, wb_ref, wg_ref, w_out_ref, ple_gate_ref, ple_w_ref,
        mu_ref, rwp_ref, lora_ref, plw_ref, conv_ref, gbias_ref, mlln_ref, lnp_ref,
        cos_ref, sin_ref, dec_ref, rtp_ref, seg_ref, ltri_ref,
        o_ref,
        zbuf, cbuf, ubuf,
        rw_r, rw_k, rw_v, rw_lw, rw_kk, rw_b, rw_y, rw_state,
        rw_rw, rw_g, rw_h, rw_dec,
        ml_q, ml_k, ml_v, ml_gt, ml_y, ml_state, ml_m,
        ml_c, ml_gb, ml_gbt, ml_sc, ml_og,
        rt_q, rt_qx, rt_k, rt_kz, rt_v, rt_y, rt_state,
        rt_r,
        mixed,
        *, alpha):
    T = x_ref.shape[0]
    n_chunks = T // CHUNK
    t_idx = pl.program_id(1)

    @pl.when(t_idx == 0)
    def _reset():
        zbuf[0:HIST, :] = jnp.zeros((HIST, RWKV_SHIFT), F32)
        cbuf[0:HIST, :] = jnp.zeros((HIST, 2 * GROUP), F32)
        ubuf[0:POOL_HIST, :] = jnp.zeros((POOL_HIST, GROUP), F32)
        rw_state[...] = jnp.zeros_like(rw_state)
        ml_state[...] = jnp.zeros_like(ml_state)
        ml_m[...] = jnp.zeros_like(ml_m)
        rt_state[...] = jnp.zeros_like(rt_state)

    x = x_ref[...]
    xb = x.astype(BF16)
    seg = seg_ref[...]
    ltri = ltri_ref[...]

    def proj(c0, width):
        if c0 == C_GATE:
            w = wg_ref[...]
        elif c0 < C_MLO:
            w = wa_ref[:, c0:c0 + width]
        else:
            w = wb_ref[:, c0 - C_MLO:c0 - C_MLO + width]
        return jnp.dot(xb, w, preferred_element_type=F32)

    zrw = proj(C_RW, RWKV_SHIFT)
    zbuf[HIST:HIST + T, :] = zrw
    prev = zbuf[HIST - 1:HIST - 1 + T, :]
    zs = zrw + (prev - zrw) * mu_ref[...]
    zbuf[0:HIST, :] = zbuf[T:T + HIST, :]
    r = zs[:, 0:GROUP]
    k = zs[:, GROUP:2 * GROUP]
    v = zs[:, 2 * GROUP:3 * GROUP]
    wa = zs[:, 3 * GROUP:RWKV_SHIFT]
    lane = lax.broadcasted_iota(jnp.int32, wa.shape, 1)
    wa = jnp.where(lane < LORA, jnp.tanh(wa), wa)
    lo = _dot(wa, lora_ref[...])
    rwp = rwp_ref[...]
    w0, a0, k_k, k_a, r_k, rw_ln_w, rw_ln_b, pl_scale = (rwp[i:i + 1, :] for i in range(8))
    w_log = -_softplus(-(w0 + lo[:, 0:GROUP])) - 0.5
    a = _sigmoid(a0 + lo[:, GROUP:2 * GROUP])
    kk = k * k_k
    kk = kk * lax.rsqrt(jnp.maximum(_segsum(kk * kk, seg), 1e-24))
    k2 = k * (1.0 + (a - 1.0) * k_a)
    rw_r[...] = r
    rw_k[...] = k2
    rw_v[...] = v
    rw_lw[...] = -jnp.exp(w_log)
    rw_kk[...] = kk
    rw_b[...] = kk * a
    bonus = _segsum(r * k2 * r_k, seg) * v

    zqk = proj(C_MLQK, 2 * GROUP)
    cbuf[HIST:HIST + T, :] = zqk
    cw = conv_ref[...]
    conv = zqk * cw[CONV_K - 1:CONV_K, :]
    for j in range(1, CONV_K):
        conv = conv + cbuf[HIST - j:HIST - j + T, :] * cw[CONV_K - 1 - j:CONV_K - j, :]
    cbuf[0:HIST, :] = cbuf[T:T + HIST, :]
    qk = _silu(conv)
    ml_q[...] = qk[:, 0:GROUP]
    ml_k[...] = qk[:, GROUP:2 * GROUP] * (HEAD_DIM ** -0.5)
    ml_v[...] = proj(C_MLV, GROUP)
    zg = proj(C_GATE, GATE_PAD) + gbias_ref[...]
    glane = lax.broadcasted_iota(jnp.int32, zg.shape, 1)
    ml_gt[...] = jnp.where(glane < N_HEADS, zg,
                           jnp.where(glane < 2 * N_HEADS, -_softplus(-zg), 0.0))

    cos = cos_ref[...]
    sin = sin_ref[...]
    cos2 = jnp.concatenate([cos, cos], axis=1)
    sin2 = jnp.concatenate([sin, sin], axis=1)
    even = lax.broadcasted_iota(jnp.int32, (T, GROUP), 1) % 2 == 0

    def rotary(z):
        partner = jnp.where(even, pltpu.roll(z, GROUP - 1, axis=1), pltpu.roll(z, 1, axis=1))
        return z * cos2 + partner * sin2

    rtp = rtp_ref[...]
    q_rot = rotary(proj(C_RTQ, GROUP))
    k_rot = rotary(proj(C_RTK, GROUP)) * (HEAD_DIM ** -0.5)
    rt_q[...] = q_rot
    rt_k[...] = k_rot
    rt_v[...] = proj(C_RTV, GROUP)
    for c in range(n_chunks):
        rows = slice(c * CHUNK, (c + 1) * CHUNK)
        rt_qx[rows, :] = q_rot[rows, :] * rtp[:, 0:GROUP]
        rt_kz[rows, :] = k_rot[rows, :] * rtp[:, GROUP:2 * GROUP]

    PAIR = PAIR_W
    ri = lax.broadcasted_iota(jnp.int32, (PAIR, PAIR), 0)
    ci = lax.broadcasted_iota(jnp.int32, (PAIR, PAIR), 1)
    same_blk = (ri // HEAD_DIM) == (ci // HEAD_DIM)
    ti = lax.broadcasted_iota(jnp.int32, (CHUNK, PAIR), 0)
    si = lax.broadcasted_iota(jnp.int32, (CHUNK, PAIR), 1) % HEAD_DIM
    tri_s = ti > si
    tri_i = ti >= si
    eye = jnp.where(ti == si, 1.0, 0.0).astype(F32)
    first = lax.broadcasted_iota(jnp.int32, (CHUNK, PAIR), 1) < HEAD_DIM
    first_row = lax.broadcasted_iota(jnp.int32, (1, 2 * PAIR), 1) % PAIR < HEAD_DIM
    e8 = jnp.where(lax.broadcasted_iota(jnp.int32, (8, GATE_PAD), 0)
                   == lax.broadcasted_iota(jnp.int32, (8, GATE_PAD), 1), 1.0, 0.0).astype(BF16)
    ones_bd = jnp.where(same_blk, 1.0, 0.0).astype(BF16)
    dec = dec_ref[...]
    g_chunk = rtp[0:1, 2 * GROUP:3 * GROUP]
    pairs = [(p, slice(p * PAIR, (p + 1) * PAIR)) for p in range(N_PAIRS)]

    def rows_of(c):
        if isinstance(c, int):
            return slice(c * CHUNK, (c + 1) * CHUNK)
        return pl.ds(pl.multiple_of(c * CHUNK, CHUNK), CHUNK)

    def gate_piece(c0, dst_ref, lo, act):
        def run():
            dst_ref[:, lo:lo + GROUP] = act(proj(c0, GROUP))
        return run

    def ple_piece(q):
        def run():
            cols = slice(q * GROUP, (q + 1) * GROUP)
            gate = _sigmoid(jnp.dot(xb, ple_gate_ref[:, cols], preferred_element_type=F32))
            o_ref[:, cols] = gate * _dot(p_ref[...], ple_w_ref[:, cols])
        return run

    fillers = [gate_piece(C_RWG, mixed, 0, _silu),
               gate_piece(C_PLU, ubuf.at[POOL_HIST:POOL_HIST + T], 0, lambda z: z),
               gate_piece(C_PLG, mixed, GROUP, _silu),
               gate_piece(C_MLO, ml_og, 0, _sigmoid),
               gate_piece(C_MLG, mixed, 2 * GROUP, _silu),
               gate_piece(C_RTG, mixed, 3 * GROUP, _silu)]
    fillers += [ple_piece(q) for q in range(D_MODEL // GROUP)]

    def fill():
        if fillers:
            fillers.pop(0)()

    def dup(x):
        return jnp.concatenate([x, x], axis=0)

    def stack(x):
        return jnp.where(same_blk, dup(x), 0.0)

    def stack2(x):
        return jnp.concatenate([stack(x[:, 0:PAIR]), stack(x[:, PAIR:2 * PAIR])], axis=1)


    def pass1(i, carry):
        cs = [i * PASS1_CHUNKS + j for j in range(PASS1_CHUNKS)]
        chains = [(j, p, ps) for j in range(PASS1_CHUNKS) for p, ps in pairs]

        prep = []
        for c in cs:
            rows = rows_of(c)
            lw = rw_lw[rows, :]
            cum = _dot_exact_lhs(ltri, lw)
            g_l = cum[CHUNK - 1:CHUNK, :]
            e_neg = jnp.exp(-cum)
            kk_c = rw_kk[rows, :]
            b_c = rw_b[rows, :]
            k_c = rw_k[rows, :]
            e_g = jnp.exp(g_l - cum)
            rw_dec[c] = jnp.broadcast_to(jnp.exp(g_l), (8, GROUP))
            prep.append(dict(al=jnp.exp(cum - lw) * kk_c, be=b_c * e_neg, kt=k_c * e_neg,
                             rt=rw_r[rows, :] * jnp.exp(cum), kh=k_c * e_g, bh=b_c * e_g,
                             v=rw_v[rows, :]))
        pick = lambda name: [prep[j][name][:, ps] for j, p, ps in chains]
        al, be, kt, rt, kh, bh, vv = (pick(n) for n in ("al", "be", "kt", "rt", "kh", "bh", "v"))
        ar = [jnp.concatenate([a, r], axis=0) for a, r in zip(al, rt)]
        nr = [_nt(x, stack(y)) for x, y in zip(ar, be)]
        pq = [_nt(x, stack(y)) for x, y in zip(ar, kt)]
        fill()
        n_m = [jnp.where(tri_s, t[0:CHUNK], 0.0) for t in nr]
        rb_m = [jnp.where(tri_i, t[CHUNK:2 * CHUNK], 0.0) for t in nr]
        p_m = [jnp.where(tri_s, t[0:CHUNK], 0.0) for t in pq]
        q_m = [jnp.where(tri_i, t[CHUNK:2 * CHUNK], 0.0) for t in pq]
        pw = [_dot(n, stack(n)) for n in n_m]
        pqv = [_dot(jnp.concatenate([x, y], axis=0), stack(v)) for x, y, v in zip(p_m, q_m, vv)]
        tinv = [eye - n for n in n_m]
        fill()

        late = []
        for c in cs:
            rows = rows_of(c)
            gt = ml_gt[rows, :]
            bcum = _dot_exact_lhs(ltri, gt)
            gl2 = lax.broadcasted_iota(jnp.int32, gt.shape, 1)
            gb = jnp.where(gl2 < N_HEADS, gt, bcum)
            gb_t = _nt_exact_lhs(e8, gb)
            ml_gb[rows, :] = gb
            ml_gbt[c] = gb_t
            b_last4 = gb_t[N_HEADS:2 * N_HEADS, CHUNK - 1:CHUNK]
            m_chunk4 = jnp.max(b_last4 - gb_t[N_HEADS:2 * N_HEADS, :] + gb_t[0:N_HEADS, :],
                               axis=-1, keepdims=True)
            ml_sc[c] = jnp.concatenate([jnp.broadcast_to(m_chunk4, (N_HEADS, 128)),
                                        jnp.broadcast_to(b_last4, (N_HEADS, 128))], axis=0)
            k_c2 = ml_k[rows, :]
            v_c2 = ml_v[rows, :]
            kz_c = rt_kz[rows, :]
            v_c3 = rt_v[rows, :]
            for p, ps in pairs:
                wg = []
                for h in (2 * p, 2 * p + 1):
                    b_col = gb[:, N_HEADS + h:N_HEADS + h + 1]
                    li_col = gb[:, h:h + 1]
                    wg.append(jnp.exp(b_last4[h:h + 1] - b_col + li_col - m_chunk4[h:h + 1]))
                wgt = jnp.where(first, wg[0], wg[1])
                late.append((c, p, k_c2[:, ps], jnp.concatenate([v_c2[:, ps] * wgt, wgt], axis=1),
                             kz_c[:, ps], v_c3[:, ps]))

        for _ in range(4):
            pw_s = [stack(x).astype(BF16) for x in pw]
            pw_next = [_dot(x, y) for x, y in zip(pw, pw_s)]
            tinv = [t + _dot(t, y) for t, y in zip(tinv, pw_s)]
            pw = pw_next
            fill()
        tinv = [t + _dot(t, stack(x)) for t, x in zip(tinv, pw)]
        fill()
        wu = [_dot(t, jnp.concatenate([stack(a), stack(x[0:CHUNK])], axis=1))
              for t, a, x in zip(tinv, al, pqv)]
        fill()
        rbwu = [_dot(x, stack2(y)) for x, y in zip(rb_m, wu)]
        gu = [_tn(y, x) for y, x in zip(wu, bh)]
        fill()
        vk = [_tn(x, y) for x, y in zip(vv, kh)]
        for c, p, k_p, vw, kz_p, v_p in late:
            cc = _tn(k_p, vw)
            ml_c[c, p] = jnp.concatenate([jnp.where(same_blk, cc[:, 0:PAIR], 0.0),
                                          jnp.where(same_blk, cc[:, PAIR:2 * PAIR], 0.0)], axis=1)
            rt_r[c, p] = jnp.where(same_blk, _tn(kz_p, v_p), 0.0)
        for n, (j, p, ps) in enumerate(chains):
            rw_g[cs[j], p] = jnp.where(same_blk, gu[n][0:PAIR], 0.0)
            rw_h[cs[j], p] = jnp.where(same_blk, vk[n] - gu[n][PAIR:2 * PAIR], 0.0)
            rows = rows_of(cs[j])
            rw_rw[rows, ps] = rt[n] - rbwu[n][:, 0:PAIR]
            rw_y[rows, ps] = pqv[n][CHUNK:2 * CHUNK] - rbwu[n][:, PAIR:2 * PAIR]
        return carry

    for i in range(n_chunks // PASS1_CHUNKS):
        pass1(i, 0)
    while fillers:
        fill()

    s_ml = [ml_state[p] for p, _ in pairs]
    m_ml = [ml_m[h:h + 1, 0:1] for h in range(N_HEADS)]
    s_rt = [rt_state[p] for p, _ in pairs]
    for c in range(n_chunks):
        for p, ps in pairs:
            s_old, s_new = [], []
            for h in (2 * p, 2 * p + 1):
                m_chunk = ml_sc[c, h:h + 1, 0:1]
                b_last = ml_sc[c, N_HEADS + h:N_HEADS + h + 1, 0:1]
                m_new = jnp.maximum(b_last + m_ml[h], m_chunk)
                s_old.append(jnp.exp(b_last + m_ml[h] - m_new))
                s_new.append(jnp.exp(m_chunk - m_new))
                ml_sc[c, h:h + 1, :] = jnp.broadcast_to(m_ml[h], (1, 128))
                m_ml[h] = m_new
            inc = ml_c[c, p]
            ml_c[c, p] = s_ml[p]
            s_ml[p] = (jnp.where(first_row, s_old[0], s_old[1]) * s_ml[p]
                       + jnp.where(first_row, s_new[0], s_new[1]) * inc)

            inc = rt_r[c, p]
            rt_r[c, p] = s_rt[p]
            s_rt[p] = g_chunk[:, ps] * s_rt[p] + inc
    for p, _ in pairs:
        ml_state[p] = s_ml[p]
        rt_state[p] = s_rt[p]
    for h in range(N_HEADS):
        ml_m[h:h + 1, :] = jnp.broadcast_to(m_ml[h], (1, 128))

    neg_inf = jnp.full((CHUNK, PAIR), -jnp.inf, F32)
    s_rw = [rw_state[p] for p, _ in pairs]
    for c in range(n_chunks):
        rows = rows_of(c)
        ld = lambda ref: [ref[rows, ps] for p, ps in pairs]
        rw_c, q_c, k_c2, v_c2, q_c3, qx_c, k_c3, v_c3 = (
            ld(ref) for ref in (rw_rw, ml_q, ml_k, ml_v, rt_q, rt_qx, rt_k, rt_v))
        y_rw = [_nt(rw_c[p], s_rw[p]) for p, _ in pairs]
        sg = [_dot(s_rw[p], rw_g[c, p]) for p, _ in pairs]
        qk_ml = [_nt(q_c[p], stack(k_c2[p])) for p, _ in pairs]
        qk_rt = [_nt(q_c3[p], stack(k_c3[p])) for p, _ in pairs]
        in_ml = [_dot(q_c[p], ml_c[c, p]) for p, _ in pairs]
        in_rt = [_dot(qx_c[p], rt_r[c, p]) for p, _ in pairs]

        gb = ml_gb[rows, :]
        gb_t = ml_gbt[c]
        sc_ml, floor, s_int = [], [], []
        for p, ps in pairs:
            h0, h1 = 2 * p, 2 * p + 1
            b_col = jnp.where(first, gb[:, N_HEADS + h0:N_HEADS + h0 + 1],
                              gb[:, N_HEADS + h1:N_HEADS + h1 + 1])
            b_row = jnp.concatenate([gb_t[N_HEADS + h0:N_HEADS + h0 + 1, :],
                                     gb_t[N_HEADS + h1:N_HEADS + h1 + 1, :]], axis=1)
            li_row = jnp.concatenate([gb_t[h0:h0 + 1, :], gb_t[h1:h1 + 1, :]], axis=1)
            d_log = jnp.where(tri_i, b_col - b_row + li_row, neg_inf)
            m_intra = jnp.where(first,
                                jnp.max(jnp.where(first, d_log, neg_inf), axis=-1, keepdims=True),
                                jnp.max(jnp.where(first, neg_inf, d_log), axis=-1, keepdims=True))
            m_inter = b_col + jnp.where(first, ml_sc[c, h0:h0 + 1, 0:1], ml_sc[c, h1:h1 + 1, 0:1])
            m_t = jnp.maximum(m_inter, m_intra)
            floor.append(jnp.exp(-m_t))
            s_int.append(jnp.exp(m_inter - m_t))
            sc_ml.append(qk_ml[p] * jnp.exp(d_log - m_t))
        sc_rt = [qk_rt[p] * dec[:, ps] for p, ps in pairs]
        tot_ml = [_dot(sc_ml[p], jnp.concatenate([stack(v_c2[p]).astype(BF16), ones_bd], axis=1))
                  for p, _ in pairs]
        intra_rt = [_dot(sc_rt[p], stack(v_c3[p])) for p, _ in pairs]
        for p, ps in pairs:
            tot = tot_ml[p] + jnp.concatenate([s_int[p], s_int[p]], axis=1) * in_ml[p]
            rw_y[rows, ps] = rw_y[rows, ps] + y_rw[p]
            ml_y[rows, ps] = tot[:, 0:PAIR] / jnp.maximum(jnp.abs(tot[:, PAIR:2 * PAIR]), floor[p])
            rt_y[rows, ps] = intra_rt[p] + in_rt[p]
            s_rw[p] = s_rw[p] * rw_dec[c, 0:1, ps] - sg[p] + rw_h[c, p]
    for p, _ in pairs:
        rw_state[p] = s_rw[p]

    y_a = (_head_norm(rw_y[...], seg, RWKV_GN_EPS) * rw_ln_w + rw_ln_b + bonus)
    mixed[:, 0:GROUP] = y_a * mixed[:, 0:GROUP]

    sums = [ubuf[...]]
    for sh in (1, 2, 4, 8):
        sums.append(sums[-1] + pltpu.roll(sums[-1], sh, axis=0))
    pgrp = lax.broadcasted_iota(jnp.int32, (T, GROUP), 1) // (GROUP // len(POOL_WINDOWS))
    wsum = sums[1][POOL_HIST:, :]
    for gi in range(1, len(POOL_WINDOWS)):
        wsum = jnp.where(pgrp == gi, sums[gi + 1][POOL_HIST:, :], wsum)
    pos = (t_idx * T + lax.broadcasted_iota(jnp.int32, (T, 1), 0) + 1).astype(F32)
    win = jnp.exp2((pgrp[0:1, :] + 1).astype(F32))
    mean = wsum * jnp.where(pos >= win, 1.0 / win, 1.0 / pos)
    ubuf[0:POOL_HIST, :] = ubuf[T:T + POOL_HIST, :]
    y_b = _dot(mean - sums[0][POOL_HIST:, :], plw_ref[...]) * pl_scale
    mixed[:, GROUP:2 * GROUP] = y_b * mixed[:, GROUP:2 * GROUP]

    hcm = ml_y[...] * ml_og[...]
    y_c = _head_norm(hcm, seg, HEAD_NORM_EPS) * mlln_ref[...]
    mixed[:, 2 * GROUP:3 * GROUP] = y_c * mixed[:, 2 * GROUP:3 * GROUP]

    y_d = _head_norm(rt_y[...], seg, HEAD_NORM_EPS)
    mixed[:, 3 * GROUP:4 * GROUP] = y_d * mixed[:, 3 * GROUP:4 * GROUP]

    lnp = lnp_ref[...]
    for r0 in range(0, T, OUT_ROWS):
        rs = slice(r0, min(r0 + OUT_ROWS, T))
        hres = alpha * x_ref[rs, :] + _dot(mixed[rs, :], w_out_ref[...]) + o_ref[rs, :]
        mu_ln = jnp.mean(hres, axis=-1, keepdims=True)
        hc = hres - mu_ln
        var = jnp.mean(hc * hc, axis=-1, keepdims=True)
        o_ref[rs, :] = hc * lax.rsqrt(var + LN_EPS) * lnp[0:1, :] + lnp[1:2, :]


def _full(shape):
    return pl.BlockSpec(shape, lambda b, t: (0,) * len(shape))


def _per_layer(arr, layer):
    nd = arr.ndim - 1
    return pl.BlockSpec((None,) + arr.shape[1:], lambda b, t: (layer,) + (0,) * nd)


def _layer_call(x, p, layer_params, shared, *, layer, alpha, tile_t):
    B, S, _ = x.shape
    T = tile_t
    grid = (B, S // T)
    tile = lambda width: pl.BlockSpec((None, T, width), lambda b, t: (b, t, 0))
    in_specs = [tile(D_MODEL),
                pl.BlockSpec((None, None, T, D_PLE), lambda b, t: (layer, b, t, 0))]
    names = ["wa", "wb", "wg", "w_out", "ple_gate", "ple_w", "mu", "rwp", "lora", "plw", "conv",
             "gbias", "mlln", "lnp"]
    in_specs += [_per_layer(layer_params[n], layer) for n in names]
    in_specs += [pl.BlockSpec((T, 2 * HEAD_DIM), lambda b, t: (t, 0))] * 2
    tail = ["dec", "rtp", "seg", "ltri"]
    in_specs += [_full(shared[n].shape) for n in tail]
    act = lambda width: pltpu.VMEM((T, width), F32)
    n_chunks = T // CHUNK
    per_chunk = lambda r, w: pltpu.VMEM((n_chunks, N_PAIRS, r, w), F32)
    scratch = [
        pltpu.VMEM((T + HIST, RWKV_SHIFT), F32),
        pltpu.VMEM((T + HIST, 2 * GROUP), F32),
        pltpu.VMEM((T + POOL_HIST, GROUP), F32),
        act(GROUP), act(GROUP), act(GROUP), act(GROUP), act(GROUP), act(GROUP), act(GROUP),
        pltpu.VMEM((N_PAIRS, PAIR_W, PAIR_W), F32),
        act(GROUP),
        per_chunk(PAIR_W, PAIR_W), per_chunk(PAIR_W, PAIR_W),
        pltpu.VMEM((n_chunks, 8, GROUP), F32),
        act(GROUP), act(GROUP), act(GROUP), act(GATE_PAD), act(GROUP),
        pltpu.VMEM((N_PAIRS, PAIR_W, 2 * PAIR_W), F32),
        pltpu.VMEM((8, 128), F32),
        per_chunk(PAIR_W, 2 * PAIR_W),
        act(GATE_PAD),
        pltpu.VMEM((n_chunks, 8, CHUNK), F32),
        pltpu.VMEM((n_chunks, 8, 128), F32),
        act(GROUP),
        act(GROUP), act(GROUP), act(GROUP), act(GROUP), act(GROUP), act(GROUP),
        pltpu.VMEM((N_PAIRS, PAIR_W, PAIR_W), F32),
        per_chunk(PAIR_W, PAIR_W),
        act(D_MODEL),
    ]
    args = ([x, p] + [layer_params[n] for n in names] + [shared["cos"], shared["sin"]]
            + [shared[n] for n in tail])
    return pl.pallas_call(
        functools.partial(_layer_kernel, alpha=alpha),
        grid=grid,
        in_specs=in_specs,
        out_specs=tile(D_MODEL),
        out_shape=jax.ShapeDtypeStruct((B, S, D_MODEL), F32),
        scratch_shapes=scratch,
        compiler_params=pltpu.CompilerParams(
            dimension_semantics=("arbitrary", "arbitrary"),
            vmem_limit_bytes=56 * 1024 * 1024),
        name="deepnorm_mixer_layer",
    )(*args)


def _block_diag(blocks):
    n = len(blocks)
    rows = []
    for i, blk in enumerate(blocks):
        z = jnp.zeros_like(blk)
        rows.append(jnp.concatenate([blk if j == i else z for j in range(n)], axis=-1))
    return jnp.concatenate(rows, axis=-2)


def kernel(x, p, w_in, rw_mu, rw_w0, rw_w2, rw_a0, rw_a2, rw_kk, rw_ka, rw_rk, rw_ln_w, rw_ln_b, pl_w, pl_scale, ml_conv, ml_ib, ml_fb, ml_ln_w, w_out, ple_w, ple_gate, ln_g, ln_b):
    B, S, D = x.shape
    depth = w_in.shape[0]
    alpha = (2.0 * depth) ** 0.25
    tile_t = min(TILE_T, S)
    assert D == D_MODEL and S % tile_t == 0 and tile_t % CHUNK == 0

    pos = jnp.arange(S, dtype=F32)
    inv_freq = ROPE_BASE ** (-jnp.arange(0, HEAD_DIM, 2, dtype=F32) / HEAD_DIM)
    ang = pos[:, None] * inv_freq[None, :]
    cos, sin = jnp.cos(ang), jnp.sin(ang)
    sgn = jnp.tile(jnp.asarray([-1.0, 1.0], F32), HEAD_DIM)
    cos_t = jnp.tile(jnp.repeat(cos, 2, axis=1), (1, 2))
    sin_t = jnp.tile(jnp.repeat(sin, 2, axis=1), (1, 2)) * sgn
    log_g = jnp.log1p(-jnp.exp2(-5.0 - jnp.arange(N_HEADS, dtype=F32)))
    cpos = jnp.arange(CHUNK, dtype=F32)
    rel = cpos[:, None] - cpos[None, :]
    dec = jnp.where(rel >= 0, jnp.exp(log_g[:, None, None] * jnp.maximum(rel, 0.0)), 0.0)
    dec = dec.transpose(1, 0, 2).reshape(CHUNK, GROUP)
    zeta = jnp.exp(log_g[:, None] * (CHUNK - 1.0 - cpos))
    xi = jnp.exp(log_g[:, None] * (cpos + 1.0))
    g_chunk = jnp.exp(log_g * CHUNK)
    lanes = lambda t: jnp.repeat(t.T, HEAD_DIM, axis=1)
    rtp = jnp.concatenate([lanes(xi), lanes(zeta),
                           jnp.broadcast_to(jnp.repeat(g_chunk, HEAD_DIM)[None, :], (CHUNK, GROUP))], axis=1)
    head_of = np.arange(GROUP) // HEAD_DIM
    shared = {
        "cos": cos_t, "sin": sin_t, "dec": dec, "rtp": rtp,
        "seg": jnp.asarray(head_of[:, None] == head_of[None, :], BF16),
        "ltri": jnp.asarray(np.tril(np.ones((CHUNK, CHUNK))), BF16),
    }

    g0 = C_MLO
    g1 = g0 + 2 * N_HEADS
    layer_params = {
        "wa": w_in[:, :, :g0].astype(BF16),
        "wb": w_in[:, :, g1:].astype(BF16),
        "wg": jnp.pad(w_in[:, :, g0:g1], ((0, 0), (0, 0), (0, GATE_PAD - 2 * N_HEADS))).astype(BF16),
        "w_out": w_out.astype(BF16),
        "ple_gate": ple_gate.astype(BF16),
        "ple_w": ple_w.astype(BF16),
        "mu": rw_mu[:, None, :].astype(F32),
        "rwp": jnp.stack([rw_w0, rw_a0, rw_kk, rw_ka, rw_rk, rw_ln_w, rw_ln_b, pl_scale], axis=1).astype(F32),
        "lora": _block_diag([rw_w2, rw_a2]).astype(BF16),
        "plw": _block_diag([pl_w[:, g] for g in range(len(POOL_WINDOWS))]).astype(BF16),
        "conv": jnp.pad(ml_conv.astype(F32), ((0, 0), (0, 8 - CONV_K), (0, 0))),
        "gbias": jnp.pad(jnp.concatenate([ml_ib, ml_fb], axis=1).astype(F32),
                         ((0, 0), (0, GATE_PAD - 2 * N_HEADS)))[:, None, :],
        "mlln": ml_ln_w[:, None, :].astype(F32),
        "lnp": jnp.stack([ln_g, ln_b], axis=1).astype(F32),
    }
    for i in range(depth):
        x = _layer_call(x, p, layer_params, shared, layer=i, alpha=alpha, tile_t=tile_t)
    return x
```

```python
import functools

import jax
import jax.numpy as jnp
import numpy as np
from jax import lax
from jax.experimental import pallas as pl
from jax.experimental.pallas import tpu as pltpu

D_MODEL = 1024
N_HEADS = 4
HEAD_DIM = 64
GROUP = 256
CHUNK = 64
PAIR_W = 2 * HEAD_DIM
N_PAIRS = GROUP // PAIR_W
LORA = 64
POOL_WINDOWS = (2, 4, 8, 16)
POOL_HIST = 16
CONV_K = 4
HIST = 8
D_PLE = 256
ROPE_BASE = 10000.0
LN_EPS = 1e-5
HEAD_NORM_EPS = 1e-6
RWKV_GN_EPS = 64e-5
RWKV_SHIFT = 3 * GROUP + 2 * LORA
GATE_PAD = 128

C_RW = 0
C_RWG = C_RW + RWKV_SHIFT
C_PLU = C_RWG + GROUP
C_PLG = C_PLU + GROUP
C_MLQK = C_PLG + GROUP
C_MLV = C_MLQK + 2 * GROUP
C_MLO = C_MLV + GROUP
C_MLG = C_MLO + GROUP
C_RTQ = C_MLG + GROUP
C_RTK = C_RTQ + GROUP
C_RTV = C_RTK + GROUP
C_RTG = C_RTV + GROUP
C_GATE = C_RTG + GROUP
N_COLS_PACKED = C_GATE + GATE_PAD

TILE_T = 512
PASS1_CHUNKS = 8
OUT_ROWS = 256
PLE_COLS = 256

F32 = jnp.float32
BF16 = jnp.bfloat16


def _dot(a, b):
    return jnp.dot(a.astype(BF16), b.astype(BF16), preferred_element_type=F32)


def _nt(a, b):
    return lax.dot_general(a.astype(BF16), b.astype(BF16), (((1,), (1,)), ((), ())),
                           preferred_element_type=F32)


def _tn(a, b):
    return lax.dot_general(a.astype(BF16), b.astype(BF16), (((0,), (0,)), ((), ())),
                           preferred_element_type=F32)


def _split3(x):
    hi = x.astype(BF16)
    r1 = x - hi.astype(F32)
    mid = r1.astype(BF16)
    lo = (r1 - mid.astype(F32)).astype(BF16)
    return hi, mid, lo


def _dot_exact_lhs(m, x):
    n = x.shape[1]
    r = jnp.dot(m, jnp.concatenate(_split3(x), axis=1), preferred_element_type=F32)
    return r[:, 0:n] + r[:, n:2 * n] + r[:, 2 * n:3 * n]


def _nt_exact_lhs(m, x):
    n = x.shape[0]
    r = lax.dot_general(m, jnp.concatenate(_split3(x), axis=0), (((1,), (1,)), ((), ())),
                        preferred_element_type=F32)
    return r[:, 0:n] + r[:, n:2 * n] + r[:, 2 * n:3 * n]


def _segsum(x, seg):
    return jnp.dot(x.astype(BF16), seg, preferred_element_type=F32)


def _head_norm(y, seg_mean, eps):
    yc = y - _segsum(y, seg_mean)
    return yc * lax.rsqrt(_segsum(yc * yc, seg_mean) + eps)


def _softplus(x):
    return jnp.maximum(x, 0.0) + jnp.log(1.0 + jnp.exp(-jnp.abs(x)))


def _sigmoid(x):
    return 0.5 * jnp.tanh(0.5 * x) + 0.5


def _silu(x):
    h = 0.5 * x
    return h * jnp.tanh(h) + h


def _layer_kernel(
        x_ref, p_ref, wa_ref, wb_ref, wg_ref, w_out_ref, ple_gate_ref, ple_w_ref,
        mu_ref, rwp_ref, lora_ref, plw_ref, conv_ref, gbias_ref, mlln_ref, lnp_ref,
        cos_ref, sin_ref, dec_ref, rtp_ref, seg_ref, ltri_ref,
        o_ref,
        zbuf, cbuf, ubuf,
        rw_r, rw_k, rw_v, rw_lw, rw_kk, rw_b, rw_y, rw_state,
        rw_rw, rw_g, rw_h, rw_dec,
        ml_q, ml_k, ml_v, ml_gt, ml_y, ml_state, ml_m,
        ml_c, ml_gb, ml_gbt, ml_sc, ml_og,
        rt_q, rt_qx, rt_k, rt_kz, rt_v, rt_y, rt_state,
        rt_r,
        mixed,
        *, alpha):
    T = x_ref.shape[0]
    n_chunks = T // CHUNK
    t_idx = pl.program_id(1)

    @pl.when(t_idx == 0)
    def _reset():
        zbuf[0:HIST, :] = jnp.zeros((HIST, RWKV_SHIFT), F32)
        cbuf[0:HIST, :] = jnp.zeros((HIST, 2 * GROUP), F32)
        ubuf[0:POOL_HIST, :] = jnp.zeros((POOL_HIST, GROUP), F32)
        rw_state[...] = jnp.zeros_like(rw_state)
        ml_state[...] = jnp.zeros_like(ml_state)
        ml_m[...] = jnp.zeros_like(ml_m)
        rt_state[...] = jnp.zeros_like(rt_state)

    x = x_ref[...]
    xb = x.astype(BF16)
    seg = seg_ref[...]
    seg_mean = (seg.astype(F32) * (1.0 / HEAD_DIM)).astype(BF16)
    ltri = ltri_ref[...]

    def proj(c0, width):
        if c0 == C_GATE:
            w = wg_ref[...]
        elif c0 < C_MLO:
            w = wa_ref[:, c0:c0 + width]
        else:
            w = wb_ref[:, c0 - C_MLO:c0 - C_MLO + width]
        return jnp.dot(xb, w, preferred_element_type=F32)

    def prep_ml_qk():
        zqk = proj(C_MLQK, 2 * GROUP)
        cbuf[HIST:HIST + T, :] = zqk
        cw = conv_ref[...]
        conv = zqk * cw[CONV_K - 1:CONV_K, :]
        for j in range(1, CONV_K):
            conv = conv + cbuf[HIST - j:HIST - j + T, :] * cw[CONV_K - 1 - j:CONV_K - j, :]
        cbuf[0:HIST, :] = cbuf[T:T + HIST, :]
        qk = _silu(conv)
        ml_q[...] = qk[:, 0:GROUP]
        ml_k[...] = qk[:, GROUP:2 * GROUP] * (HEAD_DIM ** -0.5)

    def prep_ml_v():
        ml_v[...] = proj(C_MLV, GROUP)
        zg = proj(C_GATE, GATE_PAD) + gbias_ref[...]
        glane = lax.broadcasted_iota(jnp.int32, zg.shape, 1)
        ml_gt[...] = jnp.where(glane < N_HEADS, zg,
                               jnp.where(glane < 2 * N_HEADS, -_softplus(-zg), 0.0))

    rtp = rtp_ref[...]

    def rotary(z):
        cos = cos_ref[...]
        sin = sin_ref[...]
        even = lax.broadcasted_iota(jnp.int32, (T, GROUP), 1) % 2 == 0
        partner = jnp.where(even, pltpu.roll(z, GROUP - 1, axis=1), pltpu.roll(z, 1, axis=1))
        return z * jnp.concatenate([cos, cos], axis=1) + partner * jnp.concatenate([sin, sin], axis=1)

    def prep_rt(c0, dst, dst_scaled, scale, lo):
        def run():
            rot = rotary(proj(c0, GROUP)) * scale
            dst[...] = rot
            for c in range(n_chunks):
                rows = slice(c * CHUNK, (c + 1) * CHUNK)
                dst_scaled[rows, :] = rot[rows, :] * rtp[:, lo:lo + GROUP]
        return run

    def prep_rt_v():
        rt_v[...] = proj(C_RTV, GROUP)

    def gate_piece(c0, dst_ref, lo, act):
        def run():
            dst_ref[:, lo:lo + GROUP] = act(proj(c0, GROUP))
        return run

    def ple_piece(q):
        def run():
            cols = slice(q * PLE_COLS, (q + 1) * PLE_COLS)
            gate = _sigmoid(jnp.dot(xb, ple_gate_ref[:, cols], preferred_element_type=F32))
            o_ref[:, cols] = gate * _dot(p_ref[...], ple_w_ref[:, cols])
        return run

    fillers = [prep_ml_qk, prep_ml_v,
               prep_rt(C_RTQ, rt_q, rt_qx, 1.0, 0), prep_rt(C_RTK, rt_k, rt_kz, HEAD_DIM ** -0.5, GROUP),
               prep_rt_v]
    n_late_fillers = 6
    fillers += [gate_piece(C_RWG, mixed, 0, _silu),
               gate_piece(C_PLU, ubuf.at[POOL_HIST:POOL_HIST + T], 0, lambda z: z),
               gate_piece(C_PLG, mixed, GROUP, _silu),
               gate_piece(C_MLO, ml_og, 0, _sigmoid),
               gate_piece(C_MLG, mixed, 2 * GROUP, _silu),
               gate_piece(C_RTG, mixed, 3 * GROUP, _silu)]
    ple_pieces = [ple_piece(q) for q in range(D_MODEL // PLE_COLS)]

    def fill(n=1):
        for _ in range(min(n, len(fillers))):
            fillers.pop(0)()

    zrw = proj(C_RW, RWKV_SHIFT)
    fill(2)
    zbuf[HIST:HIST + T, :] = zrw
    prev = zbuf[HIST - 1:HIST - 1 + T, :]
    zs = zrw + (prev - zrw) * mu_ref[...]
    zbuf[0:HIST, :] = zbuf[T:T + HIST, :]
    r = zs[:, 0:GROUP]
    k = zs[:, GROUP:2 * GROUP]
    v = zs[:, 2 * GROUP:3 * GROUP]
    wa = zs[:, 3 * GROUP:RWKV_SHIFT]
    lane = lax.broadcasted_iota(jnp.int32, wa.shape, 1)
    wa = jnp.where(lane < LORA, jnp.tanh(wa), wa)
    lo = _dot(wa, lora_ref[...])
    fill()
    rwp = rwp_ref[...]
    w0, a0, k_k, k_a, r_k, rw_ln_w, rw_ln_b, pl_scale = (rwp[i:i + 1, :] for i in range(8))
    w_log = -_softplus(-(w0 + lo[:, 0:GROUP])) - 0.5
    a = _sigmoid(a0 + lo[:, GROUP:2 * GROUP])
    kk = k * k_k
    kk = kk * lax.rsqrt(jnp.maximum(_segsum(kk * kk, seg), 1e-24))
    k2 = k * (1.0 + (a - 1.0) * k_a)
    rw_r[...] = r
    rw_k[...] = k2
    rw_v[...] = v
    rw_lw[...] = -jnp.exp(w_log)
    rw_kk[...] = kk
    rw_b[...] = kk * a
    bonus = _segsum(r * k2 * r_k, seg) * v

    PAIR = PAIR_W
    ri = lax.broadcasted_iota(jnp.int32, (PAIR, PAIR), 0)
    ci = lax.broadcasted_iota(jnp.int32, (PAIR, PAIR), 1)
    same_blk = (ri // HEAD_DIM) == (ci // HEAD_DIM)
    ti = lax.broadcasted_iota(jnp.int32, (CHUNK, PAIR), 0)
    si = lax.broadcasted_iota(jnp.int32, (CHUNK, PAIR), 1) % HEAD_DIM
    tri_s = ti > si
    tri_i = ti >= si
    eye = jnp.where(ti == si, 1.0, 0.0).astype(F32)
    first = lax.broadcasted_iota(jnp.int32, (CHUNK, PAIR), 1) < HEAD_DIM
    first_row = lax.broadcasted_iota(jnp.int32, (1, 2 * PAIR), 1) % PAIR < HEAD_DIM
    e8 = jnp.where(lax.broadcasted_iota(jnp.int32, (8, GATE_PAD), 0)
                   == lax.broadcasted_iota(jnp.int32, (8, GATE_PAD), 1), 1.0, 0.0).astype(BF16)
    ones_bd = jnp.where(same_blk, 1.0, 0.0).astype(BF16)
    dec = dec_ref[...]
    g_chunk = rtp[0:1, 2 * GROUP:3 * GROUP]
    pairs = [(p, slice(p * PAIR, (p + 1) * PAIR)) for p in range(N_PAIRS)]

    def rows_of(c):
        if isinstance(c, int):
            return slice(c * CHUNK, (c + 1) * CHUNK)
        return pl.ds(pl.multiple_of(c * CHUNK, CHUNK), CHUNK)

    def dup(x):
        return jnp.concatenate([x, x], axis=0)

    def stack(x):
        return jnp.where(same_blk, dup(x), 0.0)

    def stack2(x):
        return jnp.concatenate([stack(x[:, 0:PAIR]), stack(x[:, PAIR:2 * PAIR])], axis=1)


    def pass1(i, carry):
        cs = [i * PASS1_CHUNKS + j for j in range(PASS1_CHUNKS)]
        chains = [(j, p, ps) for j in range(PASS1_CHUNKS) for p, ps in pairs]

        prep = []
        for c in cs:
            rows = rows_of(c)
            lw = rw_lw[rows, :]
            cum = _dot_exact_lhs(ltri, lw)
            g_l = cum[CHUNK - 1:CHUNK, :]
            e_neg = jnp.exp(-cum)
            kk_c = rw_kk[rows, :]
            b_c = rw_b[rows, :]
            k_c = rw_k[rows, :]
            e_g = jnp.exp(g_l - cum)
            rw_dec[c] = jnp.broadcast_to(jnp.exp(g_l), (8, GROUP))
            prep.append(dict(al=jnp.exp(cum - lw) * kk_c, be=b_c * e_neg, kt=k_c * e_neg,
                             rt=rw_r[rows, :] * jnp.exp(cum), kh=k_c * e_g, bh=b_c * e_g,
                             v=rw_v[rows, :]))
        pick = lambda name: [prep[j][name][:, ps] for j, p, ps in chains]
        al, be, kt, rt, kh, bh, vv = (pick(n) for n in ("al", "be", "kt", "rt", "kh", "bh", "v"))
        ar = [jnp.concatenate([a, r], axis=0) for a, r in zip(al, rt)]
        nrpq = [_nt(x, jnp.concatenate([stack(y), stack(z)], axis=0))
                for x, y, z in zip(ar, be, kt)]
        fill(2)
        n_m = [jnp.where(tri_s, t[0:CHUNK, 0:PAIR], 0.0) for t in nrpq]
        rb_m = [jnp.where(tri_i, t[CHUNK:2 * CHUNK, 0:PAIR], 0.0) for t in nrpq]
        p_m = [jnp.where(tri_s, t[0:CHUNK, PAIR:2 * PAIR], 0.0) for t in nrpq]
        q_m = [jnp.where(tri_i, t[CHUNK:2 * CHUNK, PAIR:2 * PAIR], 0.0) for t in nrpq]
        pw = [_dot(n, stack(n)) for n in n_m]
        fill()
        pqv = [_dot(jnp.concatenate([x, y], axis=0), stack(v)) for x, y, v in zip(p_m, q_m, vv)]
        tinv = [eye - n for n in n_m]
        fill()

        def doubling_step(pw, tinv):
            both = [_dot(jnp.concatenate([x, t], axis=0), stack(x)) for x, t in zip(pw, tinv)]
            fill(2)
            return ([b[0:CHUNK] for b in both],
                    [t + b[CHUNK:2 * CHUNK] for t, b in zip(tinv, both)])

        pw, tinv = doubling_step(pw, tinv)

        while len(fillers) > n_late_fillers:
            fill()
        late = []
        for c in cs:
            rows = rows_of(c)
            gt = ml_gt[rows, :]
            bcum = _dot_exact_lhs(ltri, gt)
            gl2 = lax.broadcasted_iota(jnp.int32, gt.shape, 1)
            gb = jnp.where(gl2 < N_HEADS, gt, bcum)
            gb_t = _nt_exact_lhs(e8, gb)
            ml_gb[rows, :] = gb
            ml_gbt[c] = gb_t
            b_last4 = gb_t[N_HEADS:2 * N_HEADS, CHUNK - 1:CHUNK]
            m_chunk4 = jnp.max(b_last4 - gb_t[N_HEADS:2 * N_HEADS, :] + gb_t[0:N_HEADS, :],
                               axis=-1, keepdims=True)
            ml_sc[c] = jnp.concatenate([jnp.broadcast_to(m_chunk4, (N_HEADS, 128)),
                                        jnp.broadcast_to(b_last4, (N_HEADS, 128))], axis=0)
            k_c2 = ml_k[rows, :]
            v_c2 = ml_v[rows, :]
            kz_c = rt_kz[rows, :]
            v_c3 = rt_v[rows, :]
            for p, ps in pairs:
                wg = []
                for h in (2 * p, 2 * p + 1):
                    b_col = gb[:, N_HEADS + h:N_HEADS + h + 1]
                    li_col = gb[:, h:h + 1]
                    wg.append(jnp.exp(b_last4[h:h + 1] - b_col + li_col - m_chunk4[h:h + 1]))
                wgt = jnp.where(first, wg[0], wg[1])
                late.append((c, p, k_c2[:, ps], jnp.concatenate([v_c2[:, ps] * wgt, wgt], axis=1),
                             kz_c[:, ps], v_c3[:, ps]))

        for _ in range(3):
            pw, tinv = doubling_step(pw, tinv)
        tinv = [t + _dot(t, stack(x)) for t, x in zip(tinv, pw)]
        fill()
        wu = [_dot(t, jnp.concatenate([stack(a), stack(x[0:CHUNK])], axis=1))
              for t, a, x in zip(tinv, al, pqv)]
        fill()
        rbwu = [_dot(x, stack2(y)) for x, y in zip(rb_m, wu)]
        gu = [_tn(y, x) for y, x in zip(wu, bh)]
        fill()
        vk = [_tn(x, y) for x, y in zip(vv, kh)]
        for c, p, k_p, vw, kz_p, v_p in late:
            cc = _tn(k_p, vw)
            ml_c[c, p] = jnp.concatenate([jnp.where(same_blk, cc[:, 0:PAIR], 0.0),
                                          jnp.where(same_blk, cc[:, PAIR:2 * PAIR], 0.0)], axis=1)
            rt_r[c, p] = jnp.where(same_blk, _tn(kz_p, v_p), 0.0)
        for n, (j, p, ps) in enumerate(chains):
            rw_g[cs[j], p] = jnp.where(same_blk, gu[n][0:PAIR], 0.0)
            rw_h[cs[j], p] = jnp.where(same_blk, vk[n] - gu[n][PAIR:2 * PAIR], 0.0)
            rows = rows_of(cs[j])
            rw_rw[rows, ps] = rt[n] - rbwu[n][:, 0:PAIR]
            rw_y[rows, ps] = pqv[n][CHUNK:2 * CHUNK] - rbwu[n][:, PAIR:2 * PAIR]
        return carry

    for i in range(n_chunks // PASS1_CHUNKS):
        pass1(i, 0)
    while fillers:
        fill()

    s_ml = [ml_state[p] for p, _ in pairs]
    m_ml = [ml_m[h:h + 1, 0:1] for h in range(N_HEADS)]
    s_rt = [rt_state[p] for p, _ in pairs]
    for c in range(n_chunks):
        for p, ps in pairs:
            s_old, s_new = [], []
            for h in (2 * p, 2 * p + 1):
                m_chunk = ml_sc[c, h:h + 1, 0:1]
                b_last = ml_sc[c, N_HEADS + h:N_HEADS + h + 1, 0:1]
                m_new = jnp.maximum(b_last + m_ml[h], m_chunk)
                s_old.append(jnp.exp(b_last + m_ml[h] - m_new))
                s_new.append(jnp.exp(m_chunk - m_new))
                ml_sc[c, h:h + 1, :] = jnp.broadcast_to(m_ml[h], (1, 128))
                m_ml[h] = m_new
            inc = ml_c[c, p]
            ml_c[c, p] = s_ml[p]
            s_ml[p] = (jnp.where(first_row, s_old[0], s_old[1]) * s_ml[p]
                       + jnp.where(first_row, s_new[0], s_new[1]) * inc)

            inc = rt_r[c, p]
            rt_r[c, p] = s_rt[p]
            s_rt[p] = g_chunk[:, ps] * s_rt[p] + inc
    for p, _ in pairs:
        ml_state[p] = s_ml[p]
        rt_state[p] = s_rt[p]
    for h in range(N_HEADS):
        ml_m[h:h + 1, :] = jnp.broadcast_to(m_ml[h], (1, 128))

    neg_inf = jnp.full((CHUNK, PAIR), -jnp.inf, F32)
    s_rw = [rw_state[p] for p, _ in pairs]
    for c in range(n_chunks):
        rows = rows_of(c)
        ld = lambda ref: [ref[rows, ps] for p, ps in pairs]
        rw_c, q_c, k_c2, v_c2, q_c3, qx_c, k_c3, v_c3 = (
            ld(ref) for ref in (rw_rw, ml_q, ml_k, ml_v, rt_q, rt_qx, rt_k, rt_v))
        y_rw = [_nt(rw_c[p], s_rw[p]) for p, _ in pairs]
        sg = [_dot(s_rw[p], rw_g[c, p]) for p, _ in pairs]
        qk_ml = [_nt(q_c[p], stack(k_c2[p])) for p, _ in pairs]
        qk_rt = [_nt(q_c3[p], stack(k_c3[p])) for p, _ in pairs]
        in_ml = [_dot(q_c[p], ml_c[c, p]) for p, _ in pairs]
        in_rt = [_dot(qx_c[p], rt_r[c, p]) for p, _ in pairs]

        gb = ml_gb[rows, :]
        gb_t = ml_gbt[c]
        sc_ml, floor, s_int = [], [], []
        for p, ps in pairs:
            h0, h1 = 2 * p, 2 * p + 1
            b_col = jnp.where(first, gb[:, N_HEADS + h0:N_HEADS + h0 + 1],
                              gb[:, N_HEADS + h1:N_HEADS + h1 + 1])
            b_row = jnp.concatenate([gb_t[N_HEADS + h0:N_HEADS + h0 + 1, :],
                                     gb_t[N_HEADS + h1:N_HEADS + h1 + 1, :]], axis=1)
            li_row = jnp.concatenate([gb_t[h0:h0 + 1, :], gb_t[h1:h1 + 1, :]], axis=1)
            d_log = jnp.where(tri_i, b_col - b_row + li_row, neg_inf)
            m_intra = jnp.where(first,
                                jnp.max(jnp.where(first, d_log, neg_inf), axis=-1, keepdims=True),
                                jnp.max(jnp.where(first, neg_inf, d_log), axis=-1, keepdims=True))
            m_inter = b_col + jnp.where(first, ml_sc[c, h0:h0 + 1, 0:1], ml_sc[c, h1:h1 + 1, 0:1])
            m_t = jnp.maximum(m_inter, m_intra)
            floor.append(jnp.exp(-m_t))
            s_int.append(jnp.exp(m_inter - m_t))
            sc_ml.append(qk_ml[p] * jnp.exp(d_log - m_t))
        sc_rt = [qk_rt[p] * dec[:, ps] for p, ps in pairs]
        tot_ml = [_dot(sc_ml[p], jnp.concatenate([stack(v_c2[p]).astype(BF16), ones_bd], axis=1))
                  for p, _ in pairs]
        intra_rt = [_dot(sc_rt[p], stack(v_c3[p])) for p, _ in pairs]
        for piece in ple_pieces[c * len(ple_pieces) // n_chunks:(c + 1) * len(ple_pieces) // n_chunks]:
            piece()
        for p, ps in pairs:
            tot = tot_ml[p] + jnp.concatenate([s_int[p], s_int[p]], axis=1) * in_ml[p]
            rw_y[rows, ps] = rw_y[rows, ps] + y_rw[p]
            ml_y[rows, ps] = tot[:, 0:PAIR] / jnp.maximum(jnp.abs(tot[:, PAIR:2 * PAIR]), floor[p])
            rt_y[rows, ps] = intra_rt[p] + in_rt[p]
            s_rw[p] = s_rw[p] * rw_dec[c, 0:1, ps] - sg[p] + rw_h[c, p]
    for p, _ in pairs:
        rw_state[p] = s_rw[p]

    y_a = (_head_norm(rw_y[...], seg_mean, RWKV_GN_EPS) * rw_ln_w + rw_ln_b + bonus)
    mixed[:, 0:GROUP] = y_a * mixed[:, 0:GROUP]

    sums = [ubuf[...]]
    for sh in (1, 2, 4, 8):
        sums.append(sums[-1] + pltpu.roll(sums[-1], sh, axis=0))
    pgrp = lax.broadcasted_iota(jnp.int32, (T, GROUP), 1) // (GROUP // len(POOL_WINDOWS))
    wsum = sums[1][POOL_HIST:, :]
    for gi in range(1, len(POOL_WINDOWS)):
        wsum = jnp.where(pgrp == gi, sums[gi + 1][POOL_HIST:, :], wsum)
    pos = (t_idx * T + lax.broadcasted_iota(jnp.int32, (T, 1), 0) + 1).astype(F32)
    win = jnp.exp2((pgrp[0:1, :] + 1).astype(F32))
    mean = wsum * jnp.where(pos >= win, 1.0 / win, 1.0 / pos)
    ubuf[0:POOL_HIST, :] = ubuf[T:T + POOL_HIST, :]
    y_b = _dot(mean - sums[0][POOL_HIST:, :], plw_ref[...]) * pl_scale
    mixed[:, GROUP:2 * GROUP] = y_b * mixed[:, GROUP:2 * GROUP]

    hcm = ml_y[...] * ml_og[...]
    y_c = _head_norm(hcm, seg_mean, HEAD_NORM_EPS) * mlln_ref[...]
    mixed[:, 2 * GROUP:3 * GROUP] = y_c * mixed[:, 2 * GROUP:3 * GROUP]

    y_d = _head_norm(rt_y[...], seg_mean, HEAD_NORM_EPS)
    mixed[:, 3 * GROUP:4 * GROUP] = y_d * mixed[:, 3 * GROUP:4 * GROUP]

    lnp = lnp_ref[...]
    for r0 in range(0, T, OUT_ROWS):
        rs = slice(r0, min(r0 + OUT_ROWS, T))
        hres = alpha * x_ref[rs, :] + _dot(mixed[rs, :], w_out_ref[...]) + o_ref[rs, :]
        mu_ln = jnp.mean(hres, axis=-1, keepdims=True)
        hc = hres - mu_ln
        var = jnp.mean(hc * hc, axis=-1, keepdims=True)
        o_ref[rs, :] = hc * lax.rsqrt(var + LN_EPS) * lnp[0:1, :] + lnp[1:2, :]


def _full(shape):
    return pl.BlockSpec(shape, lambda b, t: (0,) * len(shape))


def _per_layer(arr, layer):
    nd = arr.ndim - 1
    return pl.BlockSpec((None,) + arr.shape[1:], lambda b, t: (layer,) + (0,) * nd)


def _layer_call(x, p, layer_params, shared, *, layer, alpha, tile_t):
    B, S, _ = x.shape
    T = tile_t
    grid = (B, S // T)
    tile = lambda width: pl.BlockSpec((None, T, width), lambda b, t: (b, t, 0))
    in_specs = [tile(D_MODEL),
                pl.BlockSpec((None, None, T, D_PLE), lambda b, t: (layer, b, t, 0))]
    names = ["wa", "wb", "wg", "w_out", "ple_gate", "ple_w", "mu", "rwp", "lora", "plw", "conv",
             "gbias", "mlln", "lnp"]
    in_specs += [_per_layer(layer_params[n], layer) for n in names]
    in_specs += [pl.BlockSpec((T, 2 * HEAD_DIM), lambda b, t: (t, 0))] * 2
    tail = ["dec", "rtp", "seg", "ltri"]
    in_specs += [_full(shared[n].shape) for n in tail]
    act = lambda width: pltpu.VMEM((T, width), F32)
    n_chunks = T // CHUNK
    per_chunk = lambda r, w: pltpu.VMEM((n_chunks, N_PAIRS, r, w), F32)
    scratch = [
        pltpu.VMEM((T + HIST, RWKV_SHIFT), F32),
        pltpu.VMEM((T + HIST, 2 * GROUP), F32),
        pltpu.VMEM((T + POOL_HIST, GROUP), F32),
        act(GROUP), act(GROUP), act(GROUP), act(GROUP), act(GROUP), act(GROUP), act(GROUP),
        pltpu.VMEM((N_PAIRS, PAIR_W, PAIR_W), F32),
        act(GROUP),
        per_chunk(PAIR_W, PAIR_W), per_chunk(PAIR_W, PAIR_W),
        pltpu.VMEM((n_chunks, 8, GROUP), F32),
        act(GROUP), act(GROUP), act(GROUP), act(GATE_PAD), act(GROUP),
        pltpu.VMEM((N_PAIRS, PAIR_W, 2 * PAIR_W), F32),
        pltpu.VMEM((8, 128), F32),
        per_chunk(PAIR_W, 2 * PAIR_W),
        act(GATE_PAD),
        pltpu.VMEM((n_chunks, 8, CHUNK), F32),
        pltpu.VMEM((n_chunks, 8, 128), F32),
        act(GROUP),
        act(GROUP), act(GROUP), act(GROUP), act(GROUP), act(GROUP), act(GROUP),
        pltpu.VMEM((N_PAIRS, PAIR_W, PAIR_W), F32),
        per_chunk(PAIR_W, PAIR_W),
        act(D_MODEL),
    ]
    args = ([x, p] + [layer_params[n] for n in names] + [shared["cos"], shared["sin"]]
            + [shared[n] for n in tail])
    return pl.pallas_call(
        functools.partial(_layer_kernel, alpha=alpha),
        grid=grid,
        in_specs=in_specs,
        out_specs=tile(D_MODEL),
        out_shape=jax.ShapeDtypeStruct((B, S, D_MODEL), F32),
        scratch_shapes=scratch,
        compiler_params=pltpu.CompilerParams(
            dimension_semantics=("arbitrary", "arbitrary"),
            vmem_limit_bytes=56 * 1024 * 1024),
        name="deepnorm_mixer_layer",
    )(*args)


def _block_diag(blocks):
    n = len(blocks)
    rows = []
    for i, blk in enumerate(blocks):
        z = jnp.zeros_like(blk)
        rows.append(jnp.concatenate([blk if j == i else z for j in range(n)], axis=-1))
    return jnp.concatenate(rows, axis=-2)


def kernel(x, p, w_in, rw_mu, rw_w0, rw_w2, rw_a0, rw_a2, rw_kk, rw_ka, rw_rk, rw_ln_w, rw_ln_b, pl_w, pl_scale, ml_conv, ml_ib, ml_fb, ml_ln_w, w_out, ple_w, ple_gate, ln_g, ln_b):
    B, S, D = x.shape
    depth = w_in.shape[0]
    alpha = (2.0 * depth) ** 0.25
    tile_t = min(TILE_T, S)
    assert D == D_MODEL and S % tile_t == 0 and tile_t % CHUNK == 0

    pos = jnp.arange(S, dtype=F32)
    inv_freq = ROPE_BASE ** (-jnp.arange(0, HEAD_DIM, 2, dtype=F32) / HEAD_DIM)
    ang = pos[:, None] * inv_freq[None, :]
    cos, sin = jnp.cos(ang), jnp.sin(ang)
    sgn = jnp.tile(jnp.asarray([-1.0, 1.0], F32), HEAD_DIM)
    cos_t = jnp.tile(jnp.repeat(cos, 2, axis=1), (1, 2))
    sin_t = jnp.tile(jnp.repeat(sin, 2, axis=1), (1, 2)) * sgn
    log_g = jnp.log1p(-jnp.exp2(-5.0 - jnp.arange(N_HEADS, dtype=F32)))
    cpos = jnp.arange(CHUNK, dtype=F32)
    rel = cpos[:, None] - cpos[None, :]
    dec = jnp.where(rel >= 0, jnp.exp(log_g[:, None, None] * jnp.maximum(rel, 0.0)), 0.0)
    dec = dec.transpose(1, 0, 2).reshape(CHUNK, GROUP)
    zeta = jnp.exp(log_g[:, None] * (CHUNK - 1.0 - cpos))
    xi = jnp.exp(log_g[:, None] * (cpos + 1.0))
    g_chunk = jnp.exp(log_g * CHUNK)
    lanes = lambda t: jnp.repeat(t.T, HEAD_DIM, axis=1)
    rtp = jnp.concatenate([lanes(xi), lanes(zeta),
                           jnp.broadcast_to(jnp.repeat(g_chunk, HEAD_DIM)[None, :], (CHUNK, GROUP))], axis=1)
    head_of = np.arange(GROUP) // HEAD_DIM
    shared = {
        "cos": cos_t, "sin": sin_t, "dec": dec, "rtp": rtp,
        "seg": jnp.asarray(head_of[:, None] == head_of[None, :], BF16),
        "ltri": jnp.asarray(np.tril(np.ones((CHUNK, CHUNK))), BF16),
    }

    g0 = C_MLO
    g1 = g0 + 2 * N_HEADS
    w_in_b = w_in.astype(BF16)
    layer_params = {
        "wa": w_in_b[:, :, :g0],
        "wb": w_in_b[:, :, g1:],
        "wg": jnp.pad(w_in_b[:, :, g0:g1], ((0, 0), (0, 0), (0, GATE_PAD - 2 * N_HEADS))),
        "w_out": w_out.astype(BF16),
        "ple_gate": ple_gate.astype(BF16),
        "ple_w": ple_w.astype(BF16),
        "mu": rw_mu[:, None, :].astype(F32),
        "rwp": jnp.stack([rw_w0, rw_a0, rw_kk, rw_ka, rw_rk, rw_ln_w, rw_ln_b, pl_scale], axis=1).astype(F32),
        "lora": _block_diag([rw_w2, rw_a2]).astype(BF16),
        "plw": _block_diag([pl_w[:, g] for g in range(len(POOL_WINDOWS))]).astype(BF16),
        "conv": jnp.pad(ml_conv.astype(F32), ((0, 0), (0, 8 - CONV_K), (0, 0))),
        "gbias": jnp.pad(jnp.concatenate([ml_ib, ml_fb], axis=1).astype(F32),
                         ((0, 0), (0, GATE_PAD - 2 * N_HEADS)))[:, None, :],
        "mlln": ml_ln_w[:, None, :].astype(F32),
        "lnp": jnp.stack([ln_g, ln_b], axis=1).astype(F32),
    }
    for i in range(depth):
        x = _layer_call(x, p, layer_params, shared, layer=i, alpha=alpha, tile_t=tile_t)
    return x
```

```python
import functools

import jax
import jax.numpy as jnp
import numpy as np
from jax import lax
from jax.experimental import pallas as pl
from jax.experimental.pallas import tpu as pltpu

D_MODEL = 1024
N_HEADS = 4
HEAD_DIM = 64
GROUP = 256
CHUNK = 64
PAIR_W = 2 * HEAD_DIM
N_PAIRS = GROUP // PAIR_W
LORA = 64
POOL_WINDOWS = (2, 4, 8, 16)
POOL_HIST = 16
CONV_K = 4
HIST = 8
D_PLE = 256
ROPE_BASE = 10000.0
LN_EPS = 1e-5
HEAD_NORM_EPS = 1e-6
RWKV_GN_EPS = 64e-5
RWKV_SHIFT = 3 * GROUP + 2 * LORA
GATE_PAD = 128

C_RW = 0
C_RWG = C_RW + RWKV_SHIFT
C_PLU = C_RWG + GROUP
C_PLG = C_PLU + GROUP
C_MLQK = C_PLG + GROUP
C_MLV = C_MLQK + 2 * GROUP
C_MLO = C_MLV + GROUP
C_MLG = C_MLO + GROUP
C_RTQ = C_MLG + GROUP
C_RTK = C_RTQ + GROUP
C_RTV = C_RTK + GROUP
C_RTG = C_RTV + GROUP
C_GATE = C_RTG + GROUP
N_COLS_PACKED = C_GATE + GATE_PAD

TILE_T = 512
PASS1_CHUNKS = 8
OUT_ROWS = 256
PLE_COLS = 256

F32 = jnp.float32
BF16 = jnp.bfloat16


def _dot(a, b):
    return jnp.dot(a.astype(BF16), b.astype(BF16), preferred_element_type=F32)


def _nt(a, b):
    return lax.dot_general(a.astype(BF16), b.astype(BF16), (((1,), (1,)), ((), ())),
                           preferred_element_type=F32)


def _tn(a, b):
    return lax.dot_general(a.astype(BF16), b.astype(BF16), (((0,), (0,)), ((), ())),
                           preferred_element_type=F32)


def _split3(x):
    hi = x.astype(BF16)
    r1 = x - hi.astype(F32)
    mid = r1.astype(BF16)
    lo = (r1 - mid.astype(F32)).astype(BF16)
    return hi, mid, lo


def _dot_exact_lhs(m, x):
    n = x.shape[1]
    r = jnp.dot(m, jnp.concatenate(_split3(x), axis=1), preferred_element_type=F32)
    return r[:, 0:n] + r[:, n:2 * n] + r[:, 2 * n:3 * n]


def _nt_exact_lhs(m, x):
    n = x.shape[0]
    r = lax.dot_general(m, jnp.concatenate(_split3(x), axis=0), (((1,), (1,)), ((), ())),
                        preferred_element_type=F32)
    return r[:, 0:n] + r[:, n:2 * n] + r[:, 2 * n:3 * n]


def _segsum(x, seg):
    return jnp.dot(x.astype(BF16), seg, preferred_element_type=F32)


def _head_norm(y, seg_mean, eps):
    yc = y - _segsum(y, seg_mean)
    return yc * lax.rsqrt(_segsum(yc * yc, seg_mean) + eps)


def _softplus(x):
    return jnp.maximum(x, 0.0) + jnp.log(1.0 + jnp.exp(-jnp.abs(x)))


def _sigmoid(x):
    return 0.5 * jnp.tanh(0.5 * x) + 0.5


def _silu(x):
    h = 0.5 * x
    return h * jnp.tanh(h) + h


def _layer_kernel(
        x_ref, p_ref, wa_ref, wb_ref, wg_ref, w_out_ref, ple_gate_ref, ple_w_ref,
        mu_ref, rwp_ref, lora_ref, plw_ref, conv_ref, gbias_ref, mlln_ref, lnp_ref,
        cos_ref, sin_ref, dec_ref, rtp_ref, seg_ref, ltri_ref,
        o_ref,
        zbuf, cbuf, ubuf,
        rw_r, rw_k, rw_v, rw_lw, rw_kk, rw_b, rw_y, rw_state,
        rw_rw, rw_g, rw_h, rw_dec,
        ml_q, ml_k, ml_v, ml_gt, ml_y, ml_state, ml_m,
        ml_c, ml_gb, ml_gbt, ml_sc, ml_og,
        rt_q, rt_qx, rt_k, rt_kz, rt_v, rt_y, rt_state,
        rt_r,
        mixed,
        *, alpha):
    T = x_ref.shape[0]
    n_chunks = T // CHUNK
    t_idx = pl.program_id(1)

    @pl.when(t_idx == 0)
    def _reset():
        zbuf[0:HIST, :] = jnp.zeros((HIST, RWKV_SHIFT), F32)
        cbuf[0:HIST, :] = jnp.zeros((HIST, 2 * GROUP), F32)
        ubuf[0:POOL_HIST, :] = jnp.zeros((POOL_HIST, GROUP), F32)
        rw_state[...] = jnp.zeros_like(rw_state)
        ml_state[...] = jnp.zeros_like(ml_state)
        ml_m[...] = jnp.zeros_like(ml_m)
        rt_state[...] = jnp.zeros_like(rt_state)

    x = x_ref[...]
    xb = x.astype(BF16)
    seg = seg_ref[...]
    seg_mean = (seg.astype(F32) * (1.0 / HEAD_DIM)).astype(BF16)
    ltri = ltri_ref[...]

    def proj(c0, width):
        if c0 == C_GATE:
            w = wg_ref[...]
        elif c0 < C_MLO:
            w = wa_ref[:, c0:c0 + width]
        else:
            w = wb_ref[:, c0 - C_MLO:c0 - C_MLO + width]
        return jnp.dot(xb, w, preferred_element_type=F32)

    def prep_ml_qk():
        zqk = proj(C_MLQK, 2 * GROUP)
        cbuf[HIST:HIST + T, :] = zqk
        cw = conv_ref[...]
        conv = zqk * cw[CONV_K - 1:CONV_K, :]
        for j in range(1, CONV_K):
            conv = conv + cbuf[HIST - j:HIST - j + T, :] * cw[CONV_K - 1 - j:CONV_K - j, :]
        cbuf[0:HIST, :] = cbuf[T:T + HIST, :]
        qk = _silu(conv)
        ml_q[...] = qk[:, 0:GROUP]
        ml_k[...] = qk[:, GROUP:2 * GROUP] * (HEAD_DIM ** -0.5)

    def prep_ml_v():
        ml_v[...] = proj(C_MLV, GROUP)
        zg = proj(C_GATE, GATE_PAD) + gbias_ref[...]
        glane = lax.broadcasted_iota(jnp.int32, zg.shape, 1)
        ml_gt[...] = jnp.where(glane < N_HEADS, zg,
                               jnp.where(glane < 2 * N_HEADS, -_softplus(-zg), 0.0))

    rtp = rtp_ref[...]

    def rotary(z):
        cos = cos_ref[...]
        sin = sin_ref[...]
        even = lax.broadcasted_iota(jnp.int32, (T, GROUP), 1) % 2 == 0
        partner = jnp.where(even, pltpu.roll(z, GROUP - 1, axis=1), pltpu.roll(z, 1, axis=1))
        return z * jnp.concatenate([cos, cos], axis=1) + partner * jnp.concatenate([sin, sin], axis=1)

    def prep_rt(c0, dst, dst_scaled, scale, lo):
        def run():
            rot = rotary(proj(c0, GROUP)) * scale
            dst[...] = rot
            for c in range(n_chunks):
                rows = slice(c * CHUNK, (c + 1) * CHUNK)
                dst_scaled[rows, :] = rot[rows, :] * rtp[:, lo:lo + GROUP]
        return run

    def prep_rt_v():
        rt_v[...] = proj(C_RTV, GROUP)

    def gate_piece(c0, dst_ref, lo, act):
        def run():
            dst_ref[:, lo:lo + GROUP] = act(proj(c0, GROUP))
        return run

    def ple_piece(q):
        def run():
            cols = slice(q * PLE_COLS, (q + 1) * PLE_COLS)
            gate = _sigmoid(jnp.dot(xb, ple_gate_ref[:, cols], preferred_element_type=F32))
            o_ref[:, cols] = gate * _dot(p_ref[...], ple_w_ref[:, cols])
        return run

    fillers = [prep_ml_qk, prep_ml_v,
               prep_rt(C_RTQ, rt_q, rt_qx, 1.0, 0), prep_rt(C_RTK, rt_k, rt_kz, HEAD_DIM ** -0.5, GROUP),
               prep_rt_v]
    n_late_fillers = 6
    fillers += [gate_piece(C_RWG, mixed, 0, _silu),
               gate_piece(C_PLU, ubuf.at[POOL_HIST:POOL_HIST + T], 0, lambda z: z),
               gate_piece(C_PLG, mixed, GROUP, _silu),
               gate_piece(C_MLO, ml_og, 0, _sigmoid),
               gate_piece(C_MLG, mixed, 2 * GROUP, _silu),
               gate_piece(C_RTG, mixed, 3 * GROUP, _silu)]
    ple_pieces = [ple_piece(q) for q in range(D_MODEL // PLE_COLS)]

    def fill(n=1):
        for _ in range(min(n, len(fillers))):
            fillers.pop(0)()

    zrw = proj(C_RW, RWKV_SHIFT)
    fill(2)
    zbuf[HIST:HIST + T, :] = zrw
    prev = zbuf[HIST - 1:HIST - 1 + T, :]
    zs = zrw + (prev - zrw) * mu_ref[...]
    zbuf[0:HIST, :] = zbuf[T:T + HIST, :]
    r = zs[:, 0:GROUP]
    k = zs[:, GROUP:2 * GROUP]
    v = zs[:, 2 * GROUP:3 * GROUP]
    wa = zs[:, 3 * GROUP:RWKV_SHIFT]
    lane = lax.broadcasted_iota(jnp.int32, wa.shape, 1)
    wa = jnp.where(lane < LORA, jnp.tanh(wa), wa)
    lo = _dot(wa, lora_ref[...])
    fill()
    rwp = rwp_ref[...]
    w0, a0, k_k, k_a, r_k, rw_ln_w, rw_ln_b, pl_scale = (rwp[i:i + 1, :] for i in range(8))
    w_log = -_softplus(-(w0 + lo[:, 0:GROUP])) - 0.5
    a = _sigmoid(a0 + lo[:, GROUP:2 * GROUP])
    kk = k * k_k
    kk = kk * lax.rsqrt(jnp.maximum(_segsum(kk * kk, seg), 1e-24))
    k2 = k * (1.0 + (a - 1.0) * k_a)
    rw_r[...] = r
    rw_k[...] = k2
    rw_v[...] = v
    rw_lw[...] = -jnp.exp(w_log)
    rw_kk[...] = kk
    rw_b[...] = kk * a
    bonus = _segsum(r * k2 * r_k, seg) * v

    PAIR = PAIR_W
    ri = lax.broadcasted_iota(jnp.int32, (PAIR, PAIR), 0)
    ci = lax.broadcasted_iota(jnp.int32, (PAIR, PAIR), 1)
    same_blk = (ri // HEAD_DIM) == (ci // HEAD_DIM)
    ti = lax.broadcasted_iota(jnp.int32, (CHUNK, PAIR), 0)
    si = lax.broadcasted_iota(jnp.int32, (CHUNK, PAIR), 1) % HEAD_DIM
    tri_s = ti > si
    tri_i = ti >= si
    eye = jnp.where(ti == si, 1.0, 0.0).astype(F32)

    def merge_mask(b):
        return ((ti // (2 * b)) == (si // (2 * b))) & ((ti // b) != (si // b))
    first = lax.broadcasted_iota(jnp.int32, (CHUNK, PAIR), 1) < HEAD_DIM
    first_row = lax.broadcasted_iota(jnp.int32, (1, 2 * PAIR), 1) % PAIR < HEAD_DIM
    e8 = jnp.where(lax.broadcasted_iota(jnp.int32, (8, GATE_PAD), 0)
                   == lax.broadcasted_iota(jnp.int32, (8, GATE_PAD), 1), 1.0, 0.0).astype(BF16)
    ones_bd = jnp.where(same_blk, 1.0, 0.0).astype(BF16)
    dec = dec_ref[...]
    g_chunk = rtp[0:1, 2 * GROUP:3 * GROUP]
    pairs = [(p, slice(p * PAIR, (p + 1) * PAIR)) for p in range(N_PAIRS)]

    def rows_of(c):
        if isinstance(c, int):
            return slice(c * CHUNK, (c + 1) * CHUNK)
        return pl.ds(pl.multiple_of(c * CHUNK, CHUNK), CHUNK)

    def dup(x):
        return jnp.concatenate([x, x], axis=0)

    def stack(x):
        return jnp.where(same_blk, dup(x), 0.0)

    def stack2(x):
        return jnp.concatenate([stack(x[:, 0:PAIR]), stack(x[:, PAIR:2 * PAIR])], axis=1)


    def pass1(i, carry):
        cs = [i * PASS1_CHUNKS + j for j in range(PASS1_CHUNKS)]
        chains = [(j, p, ps) for j in range(PASS1_CHUNKS) for p, ps in pairs]

        prep = []
        for c in cs:
            rows = rows_of(c)
            lw = rw_lw[rows, :]
            cum = _dot_exact_lhs(ltri, lw)
            g_l = cum[CHUNK - 1:CHUNK, :]
            e_neg = jnp.exp(-cum)
            kk_c = rw_kk[rows, :]
            b_c = rw_b[rows, :]
            k_c = rw_k[rows, :]
            e_g = jnp.exp(g_l - cum)
            rw_dec[c] = jnp.broadcast_to(jnp.exp(g_l), (8, GROUP))
            prep.append(dict(al=jnp.exp(cum - lw) * kk_c, be=b_c * e_neg, kt=k_c * e_neg,
                             rt=rw_r[rows, :] * jnp.exp(cum), kh=k_c * e_g, bh=b_c * e_g,
                             v=rw_v[rows, :]))
        pick = lambda name: [prep[j][name][:, ps] for j, p, ps in chains]
        al, be, kt, rt, kh, bh, vv = (pick(n) for n in ("al", "be", "kt", "rt", "kh", "bh", "v"))
        ar = [jnp.concatenate([a, r], axis=0) for a, r in zip(al, rt)]
        nrpq = [_nt(x, jnp.concatenate([stack(y), stack(z)], axis=0))
                for x, y, z in zip(ar, be, kt)]
        fill(2)
        n_m = [jnp.where(tri_s, t[0:CHUNK, 0:PAIR], 0.0) for t in nrpq]
        rb_m = [jnp.where(tri_i, t[CHUNK:2 * CHUNK, 0:PAIR], 0.0) for t in nrpq]
        p_m = [jnp.where(tri_s, t[0:CHUNK, PAIR:2 * PAIR], 0.0) for t in nrpq]
        q_m = [jnp.where(tri_i, t[CHUNK:2 * CHUNK, PAIR:2 * PAIR], 0.0) for t in nrpq]
        pqv = [_dot(jnp.concatenate([x, y], axis=0), stack(v)) for x, y, v in zip(p_m, q_m, vv)]
        fill()
        tinv = [eye - jnp.where(merge_mask(1), n, 0.0) for n in n_m]

        def merge_step(b, tinv):
            mask = merge_mask(b)
            dc = [_dot(t, stack(jnp.where(mask, n, 0.0))) for t, n in zip(tinv, n_m)]
            fill()
            out = [t - _dot(x, stack(t)) for t, x in zip(tinv, dc)]
            fill()
            return out

        tinv = merge_step(2, tinv)

        while len(fillers) > n_late_fillers:
            fill()
        late = []
        for c in cs:
            rows = rows_of(c)
            gt = ml_gt[rows, :]
            bcum = _dot_exact_lhs(ltri, gt)
            gl2 = lax.broadcasted_iota(jnp.int32, gt.shape, 1)
            gb = jnp.where(gl2 < N_HEADS, gt, bcum)
            gb_t = _nt_exact_lhs(e8, gb)
            ml_gb[rows, :] = gb
            ml_gbt[c] = gb_t
            b_last4 = gb_t[N_HEADS:2 * N_HEADS, CHUNK - 1:CHUNK]
            m_chunk4 = jnp.max(b_last4 - gb_t[N_HEADS:2 * N_HEADS, :] + gb_t[0:N_HEADS, :],
                               axis=-1, keepdims=True)
            ml_sc[c] = jnp.concatenate([jnp.broadcast_to(m_chunk4, (N_HEADS, 128)),
                                        jnp.broadcast_to(b_last4, (N_HEADS, 128))], axis=0)
            k_c2 = ml_k[rows, :]
            v_c2 = ml_v[rows, :]
            kz_c = rt_kz[rows, :]
            v_c3 = rt_v[rows, :]
            for p, ps in pairs:
                wg = []
                for h in (2 * p, 2 * p + 1):
                    b_col = gb[:, N_HEADS + h:N_HEADS + h + 1]
                    li_col = gb[:, h:h + 1]
                    wg.append(jnp.exp(b_last4[h:h + 1] - b_col + li_col - m_chunk4[h:h + 1]))
                wgt = jnp.where(first, wg[0], wg[1])
                late.append((c, p, k_c2[:, ps], jnp.concatenate([v_c2[:, ps] * wgt, wgt], axis=1),
                             kz_c[:, ps], v_c3[:, ps]))

        for b in (4, 8, 16, 32):
            tinv = merge_step(b, tinv)
        wu = [_dot(t, jnp.concatenate([stack(a), stack(x[0:CHUNK])], axis=1))
              for t, a, x in zip(tinv, al, pqv)]
        fill()
        rbwu = [_dot(x, stack2(y)) for x, y in zip(rb_m, wu)]
        gu = [_tn(y, x) for y, x in zip(wu, bh)]
        fill()
        vk = [_tn(x, y) for x, y in zip(vv, kh)]
        for c, p, k_p, vw, kz_p, v_p in late:
            cc = _tn(k_p, vw)
            ml_c[c, p] = jnp.concatenate([jnp.where(same_blk, cc[:, 0:PAIR], 0.0),
                                          jnp.where(same_blk, cc[:, PAIR:2 * PAIR], 0.0)], axis=1)
            rt_r[c, p] = jnp.where(same_blk, _tn(kz_p, v_p), 0.0)
        for n, (j, p, ps) in enumerate(chains):
            rw_g[cs[j], p] = jnp.where(same_blk, gu[n][0:PAIR], 0.0)
            rw_h[cs[j], p] = jnp.where(same_blk, vk[n] - gu[n][PAIR:2 * PAIR], 0.0)
            rows = rows_of(cs[j])
            rw_rw[rows, ps] = rt[n] - rbwu[n][:, 0:PAIR]
            rw_y[rows, ps] = pqv[n][CHUNK:2 * CHUNK] - rbwu[n][:, PAIR:2 * PAIR]
        return carry

    for i in range(n_chunks // PASS1_CHUNKS):
        pass1(i, 0)
    while fillers:
        fill()

    s_ml = [ml_state[p] for p, _ in pairs]
    m_ml = [ml_m[h:h + 1, 0:1] for h in range(N_HEADS)]
    s_rt = [rt_state[p] for p, _ in pairs]
    for c in range(n_chunks):
        for p, ps in pairs:
            s_old, s_new = [], []
            for h in (2 * p, 2 * p + 1):
                m_chunk = ml_sc[c, h:h + 1, 0:1]
                b_last = ml_sc[c, N_HEADS + h:N_HEADS + h + 1, 0:1]
                m_new = jnp.maximum(b_last + m_ml[h], m_chunk)
                s_old.append(jnp.exp(b_last + m_ml[h] - m_new))
                s_new.append(jnp.exp(m_chunk - m_new))
                ml_sc[c, h:h + 1, :] = jnp.broadcast_to(m_ml[h], (1, 128))
                m_ml[h] = m_new
            inc = ml_c[c, p]
            ml_c[c, p] = s_ml[p]
            s_ml[p] = (jnp.where(first_row, s_old[0], s_old[1]) * s_ml[p]
                       + jnp.where(first_row, s_new[0], s_new[1]) * inc)

            inc = rt_r[c, p]
            rt_r[c, p] = s_rt[p]
            s_rt[p] = g_chunk[:, ps] * s_rt[p] + inc
    for p, _ in pairs:
        ml_state[p] = s_ml[p]
        rt_state[p] = s_rt[p]
    for h in range(N_HEADS):
        ml_m[h:h + 1, :] = jnp.broadcast_to(m_ml[h], (1, 128))

    neg_inf = jnp.full((CHUNK, PAIR), -jnp.inf, F32)
    s_rw = [rw_state[p] for p, _ in pairs]
    for c in range(n_chunks):
        rows = rows_of(c)
        ld = lambda ref: [ref[rows, ps] for p, ps in pairs]
        rw_c, q_c, k_c2, v_c2, q_c3, qx_c, k_c3, v_c3 = (
            ld(ref) for ref in (rw_rw, ml_q, ml_k, ml_v, rt_q, rt_qx, rt_k, rt_v))
        y_rw = [_nt(rw_c[p], s_rw[p]) for p, _ in pairs]
        sg = [_dot(s_rw[p], rw_g[c, p]) for p, _ in pairs]
        qk_ml = [_nt(q_c[p], stack(k_c2[p])) for p, _ in pairs]
        qk_rt = [_nt(q_c3[p], stack(k_c3[p])) for p, _ in pairs]
        in_ml = [_dot(q_c[p], ml_c[c, p]) for p, _ in pairs]
        in_rt = [_dot(qx_c[p], rt_r[c, p]) for p, _ in pairs]

        gb = ml_gb[rows, :]
        gb_t = ml_gbt[c]
        sc_ml, floor, s_int = [], [], []
        for p, ps in pairs:
            h0, h1 = 2 * p, 2 * p + 1
            b_col = jnp.where(first, gb[:, N_HEADS + h0:N_HEADS + h0 + 1],
                              gb[:, N_HEADS + h1:N_HEADS + h1 + 1])
            b_row = jnp.concatenate([gb_t[N_HEADS + h0:N_HEADS + h0 + 1, :],
                                     gb_t[N_HEADS + h1:N_HEADS + h1 + 1, :]], axis=1)
            li_row = jnp.concatenate([gb_t[h0:h0 + 1, :], gb_t[h1:h1 + 1, :]], axis=1)
            d_log = jnp.where(tri_i, b_col - b_row + li_row, neg_inf)
            m_intra = jnp.where(first,
                                jnp.max(jnp.where(first, d_log, neg_inf), axis=-1, keepdims=True),
                                jnp.max(jnp.where(first, neg_inf, d_log), axis=-1, keepdims=True))
            m_inter = b_col + jnp.where(first, ml_sc[c, h0:h0 + 1, 0:1], ml_sc[c, h1:h1 + 1, 0:1])
            m_t = jnp.maximum(m_inter, m_intra)
            floor.append(jnp.exp(-m_t))
            s_int.append(jnp.exp(m_inter - m_t))
            sc_ml.append(qk_ml[p] * jnp.exp(d_log - m_t))
        sc_rt = [qk_rt[p] * dec[:, ps] for p, ps in pairs]
        tot_ml = [_dot(sc_ml[p], jnp.concatenate([stack(v_c2[p]).astype(BF16), ones_bd], axis=1))
                  for p, _ in pairs]
        intra_rt = [_dot(sc_rt[p], stack(v_c3[p])) for p, _ in pairs]
        for piece in ple_pieces[c * len(ple_pieces) // n_chunks:(c + 1) * len(ple_pieces) // n_chunks]:
            piece()
        for p, ps in pairs:
            tot = tot_ml[p] + jnp.concatenate([s_int[p], s_int[p]], axis=1) * in_ml[p]
            rw_y[rows, ps] = rw_y[rows, ps] + y_rw[p]
            ml_y[rows, ps] = tot[:, 0:PAIR] / jnp.maximum(jnp.abs(tot[:, PAIR:2 * PAIR]), floor[p])
            rt_y[rows, ps] = intra_rt[p] + in_rt[p]
            s_rw[p] = s_rw[p] * rw_dec[c, 0:1, ps] - sg[p] + rw_h[c, p]
    for p, _ in pairs:
        rw_state[p] = s_rw[p]

    y_a = (_head_norm(rw_y[...], seg_mean, RWKV_GN_EPS) * rw_ln_w + rw_ln_b + bonus)
    mixed[:, 0:GROUP] = y_a * mixed[:, 0:GROUP]

    sums = [ubuf[...]]
    for sh in (1, 2, 4, 8):
        sums.append(sums[-1] + pltpu.roll(sums[-1], sh, axis=0))
    pgrp = lax.broadcasted_iota(jnp.int32, (T, GROUP), 1) // (GROUP // len(POOL_WINDOWS))
    wsum = sums[1][POOL_HIST:, :]
    for gi in range(1, len(POOL_WINDOWS)):
        wsum = jnp.where(pgrp == gi, sums[gi + 1][POOL_HIST:, :], wsum)
    pos = (t_idx * T + lax.broadcasted_iota(jnp.int32, (T, 1), 0) + 1).astype(F32)
    win = jnp.exp2((pgrp[0:1, :] + 1).astype(F32))
    mean = wsum * jnp.where(pos >= win, 1.0 / win, 1.0 / pos)
    ubuf[0:POOL_HIST, :] = ubuf[T:T + POOL_HIST, :]
    y_b = _dot(mean - sums[0][POOL_HIST:, :], plw_ref[...]) * pl_scale
    mixed[:, GROUP:2 * GROUP] = y_b * mixed[:, GROUP:2 * GROUP]

    hcm = ml_y[...] * ml_og[...]
    y_c = _head_norm(hcm, seg_mean, HEAD_NORM_EPS) * mlln_ref[...]
    mixed[:, 2 * GROUP:3 * GROUP] = y_c * mixed[:, 2 * GROUP:3 * GROUP]

    y_d = _head_norm(rt_y[...], seg_mean, HEAD_NORM_EPS)
    mixed[:, 3 * GROUP:4 * GROUP] = y_d * mixed[:, 3 * GROUP:4 * GROUP]

    lnp = lnp_ref[...]
    for r0 in range(0, T, OUT_ROWS):
        rs = slice(r0, min(r0 + OUT_ROWS, T))
        hres = alpha * x_ref[rs, :] + _dot(mixed[rs, :], w_out_ref[...]) + o_ref[rs, :]
        mu_ln = jnp.mean(hres, axis=-1, keepdims=True)
        hc = hres - mu_ln
        var = jnp.mean(hc * hc, axis=-1, keepdims=True)
        o_ref[rs, :] = hc * lax.rsqrt(var + LN_EPS) * lnp[0:1, :] + lnp[1:2, :]


def _full(shape):
    return pl.BlockSpec(shape, lambda b, t: (0,) * len(shape))


def _per_layer(arr, layer):
    nd = arr.ndim - 1
    return pl.BlockSpec((None,) + arr.shape[1:], lambda b, t: (layer,) + (0,) * nd)


def _layer_call(x, p, layer_params, shared, *, layer, alpha, tile_t):
    B, S, _ = x.shape
    T = tile_t
    grid = (B, S // T)
    tile = lambda width: pl.BlockSpec((None, T, width), lambda b, t: (b, t, 0))
    in_specs = [tile(D_MODEL),
                pl.BlockSpec((None, None, T, D_PLE), lambda b, t: (layer, b, t, 0))]
    names = ["wa", "wb", "wg", "w_out", "ple_gate", "ple_w", "mu", "rwp", "lora", "plw", "conv",
             "gbias", "mlln", "lnp"]
    in_specs += [_per_layer(layer_params[n], layer) for n in names]
    in_specs += [pl.BlockSpec((T, 2 * HEAD_DIM), lambda b, t: (t, 0))] * 2
    tail = ["dec", "rtp", "seg", "ltri"]
    in_specs += [_full(shared[n].shape) for n in tail]
    act = lambda width: pltpu.VMEM((T, width), F32)
    n_chunks = T // CHUNK
    per_chunk = lambda r, w: pltpu.VMEM((n_chunks, N_PAIRS, r, w), F32)
    scratch = [
        pltpu.VMEM((T + HIST, RWKV_SHIFT), F32),
        pltpu.VMEM((T + HIST, 2 * GROUP), F32),
        pltpu.VMEM((T + POOL_HIST, GROUP), F32),
        act(GROUP), act(GROUP), act(GROUP), act(GROUP), act(GROUP), act(GROUP), act(GROUP),
        pltpu.VMEM((N_PAIRS, PAIR_W, PAIR_W), F32),
        act(GROUP),
        per_chunk(PAIR_W, PAIR_W), per_chunk(PAIR_W, PAIR_W),
        pltpu.VMEM((n_chunks, 8, GROUP), F32),
        act(GROUP), act(GROUP), act(GROUP), act(GATE_PAD), act(GROUP),
        pltpu.VMEM((N_PAIRS, PAIR_W, 2 * PAIR_W), F32),
        pltpu.VMEM((8, 128), F32),
        per_chunk(PAIR_W, 2 * PAIR_W),
        act(GATE_PAD),
        pltpu.VMEM((n_chunks, 8, CHUNK), F32),
        pltpu.VMEM((n_chunks, 8, 128), F32),
        act(GROUP),
        act(GROUP), act(GROUP), act(GROUP), act(GROUP), act(GROUP), act(GROUP),
        pltpu.VMEM((N_PAIRS, PAIR_W, PAIR_W), F32),
        per_chunk(PAIR_W, PAIR_W),
        act(D_MODEL),
    ]
    args = ([x, p] + [layer_params[n] for n in names] + [shared["cos"], shared["sin"]]
            + [shared[n] for n in tail])
    return pl.pallas_call(
        functools.partial(_layer_kernel, alpha=alpha),
        grid=grid,
        in_specs=in_specs,
        out_specs=tile(D_MODEL),
        out_shape=jax.ShapeDtypeStruct((B, S, D_MODEL), F32),
        scratch_shapes=scratch,
        compiler_params=pltpu.CompilerParams(
            dimension_semantics=("arbitrary", "arbitrary"),
            vmem_limit_bytes=56 * 1024 * 1024),
        name="deepnorm_mixer_layer",
    )(*args)


def _block_diag(blocks):
    n = len(blocks)
    rows = []
    for i, blk in enumerate(blocks):
        z = jnp.zeros_like(blk)
        rows.append(jnp.concatenate([blk if j == i else z for j in range(n)], axis=-1))
    return jnp.concatenate(rows, axis=-2)


def kernel(x, p, w_in, rw_mu, rw_w0, rw_w2, rw_a0, rw_a2, rw_kk, rw_ka, rw_rk, rw_ln_w, rw_ln_b, pl_w, pl_scale, ml_conv, ml_ib, ml_fb, ml_ln_w, w_out, ple_w, ple_gate, ln_g, ln_b):
    B, S, D = x.shape
    depth = w_in.shape[0]
    alpha = (2.0 * depth) ** 0.25
    tile_t = min(TILE_T, S)
    assert D == D_MODEL and S % tile_t == 0 and tile_t % CHUNK == 0

    pos = jnp.arange(S, dtype=F32)
    inv_freq = ROPE_BASE ** (-jnp.arange(0, HEAD_DIM, 2, dtype=F32) / HEAD_DIM)
    ang = pos[:, None] * inv_freq[None, :]
    cos, sin = jnp.cos(ang), jnp.sin(ang)
    sgn = jnp.tile(jnp.asarray([-1.0, 1.0], F32), HEAD_DIM)
    cos_t = jnp.tile(jnp.repeat(cos, 2, axis=1), (1, 2))
    sin_t = jnp.tile(jnp.repeat(sin, 2, axis=1), (1, 2)) * sgn
    log_g = jnp.log1p(-jnp.exp2(-5.0 - jnp.arange(N_HEADS, dtype=F32)))
    cpos = jnp.arange(CHUNK, dtype=F32)
    rel = cpos[:, None] - cpos[None, :]
    dec = jnp.where(rel >= 0, jnp.exp(log_g[:, None, None] * jnp.maximum(rel, 0.0)), 0.0)
    dec = dec.transpose(1, 0, 2).reshape(CHUNK, GROUP)
    zeta = jnp.exp(log_g[:, None] * (CHUNK - 1.0 - cpos))
    xi = jnp.exp(log_g[:, None] * (cpos + 1.0))
    g_chunk = jnp.exp(log_g * CHUNK)
    lanes = lambda t: jnp.repeat(t.T, HEAD_DIM, axis=1)
    rtp = jnp.concatenate([lanes(xi), lanes(zeta),
                           jnp.broadcast_to(jnp.repeat(g_chunk, HEAD_DIM)[None, :], (CHUNK, GROUP))], axis=1)
    head_of = np.arange(GROUP) // HEAD_DIM
    shared = {
        "cos": cos_t, "sin": sin_t, "dec": dec, "rtp": rtp,
        "seg": jnp.asarray(head_of[:, None] == head_of[None, :], BF16),
        "ltri": jnp.asarray(np.tril(np.ones((CHUNK, CHUNK))), BF16),
    }

    g0 = C_MLO
    g1 = g0 + 2 * N_HEADS
    w_in_b = w_in.astype(BF16)
    layer_params = {
        "wa": w_in_b[:, :, :g0],
        "wb": w_in_b[:, :, g1:],
        "wg": jnp.pad(w_in_b[:, :, g0:g1], ((0, 0), (0, 0), (0, GATE_PAD - 2 * N_HEADS))),
        "w_out": w_out.astype(BF16),
        "ple_gate": ple_gate.astype(BF16),
        "ple_w": ple_w.astype(BF16),
        "mu": rw_mu[:, None, :].astype(F32),
        "rwp": jnp.stack([rw_w0, rw_a0, rw_kk, rw_ka, rw_rk, rw_ln_w, rw_ln_b, pl_scale], axis=1).astype(F32),
        "lora": _block_diag([rw_w2, rw_a2]).astype(BF16),
        "plw": _block_diag([pl_w[:, g] for g in range(len(POOL_WINDOWS))]).astype(BF16),
        "conv": jnp.pad(ml_conv.astype(F32), ((0, 0), (0, 8 - CONV_K), (0, 0))),
        "gbias": jnp.pad(jnp.concatenate([ml_ib, ml_fb], axis=1).astype(F32),
                         ((0, 0), (0, GATE_PAD - 2 * N_HEADS)))[:, None, :],
        "mlln": ml_ln_w[:, None, :].astype(F32),
        "lnp": jnp.stack([ln_g, ln_b], axis=1).astype(F32),
    }
    for i in range(depth):
        x = _layer_call(x, p, layer_params, shared, layer=i, alpha=alpha, tile_t=tile_t)
    return x
```

```python
import functools

import jax
import jax.numpy as jnp
import numpy as np
from jax import lax
from jax.experimental import pallas as pl
from jax.experimental.pallas import tpu as pltpu

D_MODEL = 1024
N_HEADS = 4
HEAD_DIM = 64
GROUP = 256
CHUNK = 64
PAIR_W = 2 * HEAD_DIM
N_PAIRS = GROUP // PAIR_W
LORA = 64
POOL_WINDOWS = (2, 4, 8, 16)
POOL_HIST = 16
CONV_K = 4
HIST = 8
D_PLE = 256
ROPE_BASE = 10000.0
LN_EPS = 1e-5
HEAD_NORM_EPS = 1e-6
RWKV_GN_EPS = 64e-5
RWKV_SHIFT = 3 * GROUP + 2 * LORA
GATE_PAD = 128

C_RW = 0
C_RWG = C_RW + RWKV_SHIFT
C_PLU = C_RWG + GROUP
C_PLG = C_PLU + GROUP
C_MLQK = C_PLG + GROUP
C_MLV = C_MLQK + 2 * GROUP
C_MLO = C_MLV + GROUP
C_MLG = C_MLO + GROUP
C_RTQ = C_MLG + GROUP
C_RTK = C_RTQ + GROUP
C_RTV = C_RTK + GROUP
C_RTG = C_RTV + GROUP
C_GATE = C_RTG + GROUP
N_COLS_PACKED = C_GATE + GATE_PAD

TILE_T = 512
PASS1_CHUNKS = 8
OUT_ROWS = 256
PLE_COLS = 256

F32 = jnp.float32
BF16 = jnp.bfloat16


def _dot(a, b):
    return jnp.dot(a.astype(BF16), b.astype(BF16), preferred_element_type=F32)


def _nt(a, b):
    return lax.dot_general(a.astype(BF16), b.astype(BF16), (((1,), (1,)), ((), ())),
                           preferred_element_type=F32)


def _tn(a, b):
    return lax.dot_general(a.astype(BF16), b.astype(BF16), (((0,), (0,)), ((), ())),
                           preferred_element_type=F32)


def _split3(x):
    hi = x.astype(BF16)
    r1 = x - hi.astype(F32)
    mid = r1.astype(BF16)
    lo = (r1 - mid.astype(F32)).astype(BF16)
    return hi, mid, lo


def _dot_exact_lhs(m, x):
    n = x.shape[1]
    r = jnp.dot(m, jnp.concatenate(_split3(x), axis=1), preferred_element_type=F32)
    return r[:, 0:n] + r[:, n:2 * n] + r[:, 2 * n:3 * n]


def _nt_exact_lhs(m, x):
    n = x.shape[0]
    r = lax.dot_general(m, jnp.concatenate(_split3(x), axis=0), (((1,), (1,)), ((), ())),
                        preferred_element_type=F32)
    return r[:, 0:n] + r[:, n:2 * n] + r[:, 2 * n:3 * n]


def _segsum(x, seg):
    return jnp.dot(x.astype(BF16), seg, preferred_element_type=F32)


def _head_norm(y, seg_mean, eps):
    yc = y - _segsum(y, seg_mean)
    return yc * lax.rsqrt(_segsum(yc * yc, seg_mean) + eps)


def _softplus(x):
    return jnp.maximum(x, 0.0) + jnp.log(1.0 + jnp.exp(-jnp.abs(x)))


def _sigmoid(x):
    return 0.5 * jnp.tanh(0.5 * x) + 0.5


def _silu(x):
    h = 0.5 * x
    return h * jnp.tanh(h) + h


def _layer_kernel(
        x_ref, p_ref, wr_ref, wa_ref, wb_ref, w_out_ref, ple_gate_ref, ple_w_ref,
        mu_ref, rwp_ref, lora_ref, plw_ref, conv_ref, gbias_ref, mlln_ref, lnp_ref,
        cs_ref, dec_ref, rtp_ref, seg_ref, ltri_ref,
        o_ref,
        zbuf, cbuf, ubuf,
        rw_r, rw_k, rw_v, rw_lw, rw_kk, rw_b, rw_y, rw_state,
        rw_rw, rw_g, rw_h, rw_dec,
        ml_q, ml_k, ml_v, ml_gt, ml_y, ml_state, ml_m,
        ml_c, ml_gb, ml_gbt, ml_sc, ml_og,
        rt_q, rt_qx, rt_k, rt_kz, rt_v, rt_y, rt_state,
        rt_r,
        mixed, wb_s,
        *, alpha):
    T = x_ref.shape[0]
    n_chunks = T // CHUNK
    t_idx = pl.program_id(1)

    @pl.when(t_idx == 0)
    def _reset():
        zbuf[0:HIST, :] = jnp.zeros((HIST, RWKV_SHIFT), F32)
        cbuf[0:HIST, :] = jnp.zeros((HIST, 2 * GROUP), F32)
        ubuf[0:POOL_HIST, :] = jnp.zeros((POOL_HIST, GROUP), F32)
        rw_state[...] = jnp.zeros_like(rw_state)
        ml_state[...] = jnp.zeros_like(ml_state)
        ml_m[...] = jnp.zeros_like(ml_m)
        rt_state[...] = jnp.zeros_like(rt_state)

    @pl.when((pl.program_id(0) == 0) & (t_idx == 0))
    def _align_wb():
        wb_s[...] = wb_ref[:, 2 * N_HEADS:2 * N_HEADS + wb_s.shape[1]]

    x = x_ref[...]
    xb = x.astype(BF16)
    seg = seg_ref[...]
    seg_mean = (seg.astype(F32) * (1.0 / HEAD_DIM)).astype(BF16)
    ltri = ltri_ref[...]

    def proj(c0, width):
        if c0 < C_MLO:
            w = wa_ref[:, c0 - C_RWG:c0 - C_RWG + width]
        else:
            w = wb_s[:, c0 - C_MLO:c0 - C_MLO + width]
        return jnp.dot(xb, w, preferred_element_type=F32)

    def prep_ml_qk():
        zqk = proj(C_MLQK, 2 * GROUP)
        cbuf[HIST:HIST + T, :] = zqk
        cw = conv_ref[...]
        conv = zqk * cw[CONV_K - 1:CONV_K, :]
        for j in range(1, CONV_K):
            conv = conv + cbuf[HIST - j:HIST - j + T, :] * cw[CONV_K - 1 - j:CONV_K - j, :]
        cbuf[0:HIST, :] = cbuf[T:T + HIST, :]
        qk = _silu(conv)
        ml_q[...] = qk[:, 0:GROUP].astype(BF16)
        ml_k[...] = (qk[:, GROUP:2 * GROUP] * (HEAD_DIM ** -0.5)).astype(BF16)

    def prep_ml_v():
        ml_v[...] = proj(C_MLV, GROUP).astype(BF16)
        zg = ml_gt[...] + gbias_ref[...]
        glane = lax.broadcasted_iota(jnp.int32, zg.shape, 1)
        ml_gt[...] = jnp.where(glane < N_HEADS, zg,
                               jnp.where(glane < 2 * N_HEADS, -_softplus(-zg), 0.0))

    rtp = rtp_ref[...]

    def rotary(z):
        cos = cs_ref[:, 0:PAIR_W]
        sin = cs_ref[:, PAIR_W:2 * PAIR_W]
        even = lax.broadcasted_iota(jnp.int32, (T, GROUP), 1) % 2 == 0
        partner = jnp.where(even, pltpu.roll(z, GROUP - 1, axis=1), pltpu.roll(z, 1, axis=1))
        return z * jnp.concatenate([cos, cos], axis=1) + partner * jnp.concatenate([sin, sin], axis=1)

    def prep_rt(c0, dst, dst_scaled, scale, lo):
        def run():
            rot = rotary(proj(c0, GROUP)) * scale
            dst[...] = rot.astype(BF16)
            for c in range(n_chunks):
                rows = slice(c * CHUNK, (c + 1) * CHUNK)
                dst_scaled[rows, :] = (rot[rows, :] * rtp[:, lo:lo + GROUP]).astype(BF16)
        return run

    def prep_rt_v():
        rt_v[...] = proj(C_RTV, GROUP).astype(BF16)

    def gate_piece(c0, dst_ref, lo, act):
        def run():
            dst_ref[:, lo:lo + GROUP] = act(proj(c0, GROUP))
        return run

    def ple_piece(q):
        def run():
            cols = slice(q * PLE_COLS, (q + 1) * PLE_COLS)
            gate = _sigmoid(jnp.dot(xb, ple_gate_ref[:, cols], preferred_element_type=F32))
            o_ref[:, cols] = gate * _dot(p_ref[...], ple_w_ref[:, cols])
        return run

    fillers = [prep_ml_qk, prep_ml_v,
               prep_rt(C_RTQ, rt_q, rt_qx, 1.0, 0), prep_rt(C_RTK, rt_k, rt_kz, HEAD_DIM ** -0.5, GROUP),
               prep_rt_v]
    n_late_fillers = 6
    fillers += [gate_piece(C_RWG, mixed, 0, _silu),
               gate_piece(C_PLU, ubuf.at[POOL_HIST:POOL_HIST + T], 0, lambda z: z),
               gate_piece(C_PLG, mixed, GROUP, _silu),
               gate_piece(C_MLO, ml_og, 0, _sigmoid),
               gate_piece(C_MLG, mixed, 2 * GROUP, _silu),
               gate_piece(C_RTG, mixed, 3 * GROUP, _silu)]
    ple_pieces = [ple_piece(q) for q in range(D_MODEL // PLE_COLS)]

    def fill(n=1):
        for _ in range(min(n, len(fillers))):
            fillers.pop(0)()

    z1 = jnp.dot(xb, wr_ref[...], preferred_element_type=F32)
    zrw = z1[:, 0:RWKV_SHIFT]
    ml_gt[...] = z1[:, RWKV_SHIFT:RWKV_SHIFT + GATE_PAD]
    fill(2)
    zbuf[HIST:HIST + T, :] = zrw
    prev = zbuf[HIST - 1:HIST - 1 + T, :]
    zs = zrw + (prev - zrw) * mu_ref[...]
    zbuf[0:HIST, :] = zbuf[T:T + HIST, :]
    r = zs[:, 0:GROUP]
    k = zs[:, GROUP:2 * GROUP]
    v = zs[:, 2 * GROUP:3 * GROUP]
    wa = zs[:, 3 * GROUP:RWKV_SHIFT]
    lane = lax.broadcasted_iota(jnp.int32, wa.shape, 1)
    wa = jnp.where(lane < LORA, jnp.tanh(wa), wa)
    lo = _dot(wa, lora_ref[...])
    fill()
    rwp = rwp_ref[...]
    w0, a0, k_k, k_a, r_k, rw_ln_w, rw_ln_b, pl_scale = (rwp[i:i + 1, :] for i in range(8))
    w_log = -_softplus(-(w0 + lo[:, 0:GROUP])) - 0.5
    a = _sigmoid(a0 + lo[:, GROUP:2 * GROUP])
    kk = k * k_k
    kk = kk * lax.rsqrt(jnp.maximum(_segsum(kk * kk, seg), 1e-24))
    k2 = k * (1.0 + (a - 1.0) * k_a)
    rw_r[...] = r
    rw_k[...] = k2
    rw_v[...] = v.astype(BF16)
    rw_lw[...] = -jnp.exp(w_log)
    rw_kk[...] = kk
    rw_b[...] = kk * a
    bonus = _segsum(r * k2 * r_k, seg) * v

    PAIR = PAIR_W
    ri = lax.broadcasted_iota(jnp.int32, (PAIR, PAIR), 0)
    ci = lax.broadcasted_iota(jnp.int32, (PAIR, PAIR), 1)
    same_blk = (ri // HEAD_DIM) == (ci // HEAD_DIM)
    ti = lax.broadcasted_iota(jnp.int32, (CHUNK, PAIR), 0)
    si = lax.broadcasted_iota(jnp.int32, (CHUNK, PAIR), 1) % HEAD_DIM
    tri_s = ti > si
    tri_i = ti >= si
    eye = jnp.where(ti == si, 1.0, 0.0).astype(F32)

    def merge_mask(b):
        return ((ti // (2 * b)) == (si // (2 * b))) & ((ti // b) != (si // b))
    first = lax.broadcasted_iota(jnp.int32, (CHUNK, PAIR), 1) < HEAD_DIM
    first_row = lax.broadcasted_iota(jnp.int32, (1, 2 * PAIR), 1) % PAIR < HEAD_DIM
    e8 = jnp.where(lax.broadcasted_iota(jnp.int32, (8, GATE_PAD), 0)
                   == lax.broadcasted_iota(jnp.int32, (8, GATE_PAD), 1), 1.0, 0.0).astype(BF16)
    ones_bd = jnp.where(same_blk, 1.0, 0.0).astype(BF16)
    dec = dec_ref[...]
    g_chunk = rtp[0:1, 2 * GROUP:3 * GROUP]
    pairs = [(p, slice(p * PAIR, (p + 1) * PAIR)) for p in range(N_PAIRS)]

    def rows_of(c):
        if isinstance(c, int):
            return slice(c * CHUNK, (c + 1) * CHUNK)
        return pl.ds(pl.multiple_of(c * CHUNK, CHUNK), CHUNK)

    def dup(x):
        return jnp.concatenate([x, x], axis=0)

    def stack(x):
        return jnp.where(same_blk, dup(x), 0.0)

    def stack2(x):
        return jnp.concatenate([stack(x[:, 0:PAIR]), stack(x[:, PAIR:2 * PAIR])], axis=1)


    def state_scans():
        s_ml = [ml_state[p] for p, _ in pairs]
        m_ml = [ml_m[h:h + 1, 0:1] for h in range(N_HEADS)]
        s_rt = [rt_state[p] for p, _ in pairs]
        for c in range(n_chunks):
            for p, ps in pairs:
                s_old, s_new = [], []
                for h in (2 * p, 2 * p + 1):
                    m_chunk = ml_sc[c, h:h + 1, 0:1]
                    b_last = ml_sc[c, N_HEADS + h:N_HEADS + h + 1, 0:1]
                    m_new = jnp.maximum(b_last + m_ml[h], m_chunk)
                    s_old.append(jnp.exp(b_last + m_ml[h] - m_new))
                    s_new.append(jnp.exp(m_chunk - m_new))
                    ml_sc[c, h:h + 1, :] = jnp.broadcast_to(m_ml[h], (1, 128))
                    m_ml[h] = m_new
                inc = ml_c[c, p]
                ml_c[c, p] = s_ml[p]
                s_ml[p] = (jnp.where(first_row, s_old[0], s_old[1]) * s_ml[p]
                           + jnp.where(first_row, s_new[0], s_new[1]) * inc)

                inc = rt_r[c, p]
                rt_r[c, p] = s_rt[p]
                s_rt[p] = g_chunk[:, ps] * s_rt[p] + inc
        for p, _ in pairs:
            ml_state[p] = s_ml[p]
            rt_state[p] = s_rt[p]
        for h in range(N_HEADS):
            ml_m[h:h + 1, :] = jnp.broadcast_to(m_ml[h], (1, 128))

    def pass1(i, carry):
        cs = [i * PASS1_CHUNKS + j for j in range(PASS1_CHUNKS)]
        chains = [(j, p, ps) for j in range(PASS1_CHUNKS) for p, ps in pairs]

        prep = []
        for c in cs:
            rows = rows_of(c)
            lw = rw_lw[rows, :]
            cum = _dot_exact_lhs(ltri, lw)
            g_l = cum[CHUNK - 1:CHUNK, :]
            e_neg = jnp.exp(-cum)
            kk_c = rw_kk[rows, :]
            b_c = rw_b[rows, :]
            k_c = rw_k[rows, :]
            e_g = jnp.exp(g_l - cum)
            rw_dec[c] = jnp.broadcast_to(jnp.exp(g_l), (8, GROUP))
            prep.append(dict(al=jnp.exp(cum - lw) * kk_c, be=b_c * e_neg, kt=k_c * e_neg,
                             rt=rw_r[rows, :] * jnp.exp(cum), kh=k_c * e_g, bh=b_c * e_g,
                             v=rw_v[rows, :]))
        pick = lambda name: [prep[j][name][:, ps] for j, p, ps in chains]
        al, be, kt, rt, kh, bh, vv = (pick(n) for n in ("al", "be", "kt", "rt", "kh", "bh", "v"))
        ar = [jnp.concatenate([a, r], axis=0) for a, r in zip(al, rt)]
        nrpq = [_nt(x, jnp.concatenate([stack(y), stack(z)], axis=0))
                for x, y, z in zip(ar, be, kt)]
        fill(2)
        n_m = [jnp.where(tri_s, t[0:CHUNK, 0:PAIR], 0.0) for t in nrpq]
        rb_m = [jnp.where(tri_i, t[CHUNK:2 * CHUNK, 0:PAIR], 0.0) for t in nrpq]
        p_m = [jnp.where(tri_s, t[0:CHUNK, PAIR:2 * PAIR], 0.0) for t in nrpq]
        q_m = [jnp.where(tri_i, t[CHUNK:2 * CHUNK, PAIR:2 * PAIR], 0.0) for t in nrpq]
        pqv = [_dot(jnp.concatenate([x, y], axis=0), stack(v)) for x, y, v in zip(p_m, q_m, vv)]
        fill()
        tinv = [eye - jnp.where(merge_mask(1), n, 0.0) for n in n_m]

        def merge_step(b, tinv):
            mask = merge_mask(b)
            dc = [_dot(t, stack(jnp.where(mask, n, 0.0))) for t, n in zip(tinv, n_m)]
            fill()
            out = [t - _dot(x, stack(t)) for t, x in zip(tinv, dc)]
            fill()
            return out

        tinv = merge_step(2, tinv)

        while len(fillers) > n_late_fillers:
            fill()
        late = []
        for c in cs:
            rows = rows_of(c)
            gt = ml_gt[rows, :]
            bcum = _dot_exact_lhs(ltri, gt)
            gl2 = lax.broadcasted_iota(jnp.int32, gt.shape, 1)
            gb = jnp.where(gl2 < N_HEADS, gt, bcum)
            gb_t = _nt_exact_lhs(e8, gb)
            ml_gb[rows, :] = gb
            ml_gbt[c] = gb_t
            b_last4 = gb_t[N_HEADS:2 * N_HEADS, CHUNK - 1:CHUNK]
            m_chunk4 = jnp.max(b_last4 - gb_t[N_HEADS:2 * N_HEADS, :] + gb_t[0:N_HEADS, :],
                               axis=-1, keepdims=True)
            ml_sc[c] = jnp.concatenate([jnp.broadcast_to(m_chunk4, (N_HEADS, 128)),
                                        jnp.broadcast_to(b_last4, (N_HEADS, 128))], axis=0)
            k_c2 = ml_k[rows, :]
            v_c2 = ml_v[rows, :]
            kz_c = rt_kz[rows, :]
            v_c3 = rt_v[rows, :]
            for p, ps in pairs:
                wg = []
                for h in (2 * p, 2 * p + 1):
                    b_col = gb[:, N_HEADS + h:N_HEADS + h + 1]
                    li_col = gb[:, h:h + 1]
                    wg.append(jnp.exp(b_last4[h:h + 1] - b_col + li_col - m_chunk4[h:h + 1]))
                wgt = jnp.where(first, wg[0], wg[1])
                late.append((c, p, k_c2[:, ps], jnp.concatenate([v_c2[:, ps] * wgt, wgt], axis=1),
                             kz_c[:, ps], v_c3[:, ps]))

        for b in (4, 8):
            tinv = merge_step(b, tinv)
        for c, p, k_p, vw, kz_p, v_p in late:
            cc = _tn(k_p, vw)
            ml_c[c, p] = jnp.concatenate([jnp.where(same_blk, cc[:, 0:PAIR], 0.0),
                                          jnp.where(same_blk, cc[:, PAIR:2 * PAIR], 0.0)], axis=1)
            rt_r[c, p] = jnp.where(same_blk, _tn(kz_p, v_p), 0.0)
        state_scans()
        for b in (16, 32):
            tinv = merge_step(b, tinv)
        wu = [_dot(t, jnp.concatenate([stack(a), stack(x[0:CHUNK])], axis=1))
              for t, a, x in zip(tinv, al, pqv)]
        fill()
        rbwu = [_dot(x, stack2(y)) for x, y in zip(rb_m, wu)]
        gu = [_tn(y, x) for y, x in zip(wu, bh)]
        fill()
        vk = [_tn(x, y) for x, y in zip(vv, kh)]
        for n, (j, p, ps) in enumerate(chains):
            rw_g[cs[j], p] = jnp.where(same_blk, gu[n][0:PAIR], 0.0)
            rw_h[cs[j], p] = jnp.where(same_blk, vk[n] - gu[n][PAIR:2 * PAIR], 0.0)
            rows = rows_of(cs[j])
            rw_rw[rows, ps] = rt[n] - rbwu[n][:, 0:PAIR]
            rw_y[rows, ps] = pqv[n][CHUNK:2 * CHUNK] - rbwu[n][:, PAIR:2 * PAIR]
        return carry

    for i in range(n_chunks // PASS1_CHUNKS):
        pass1(i, 0)
    while fillers:
        fill()

    neg_inf = jnp.full((CHUNK, PAIR), -jnp.inf, F32)
    s_rw = [rw_state[p] for p, _ in pairs]
    for c in range(n_chunks):
        rows = rows_of(c)
        ld = lambda ref: [ref[rows, ps] for p, ps in pairs]
        rw_c, q_c, k_c2, v_c2, q_c3, qx_c, k_c3, v_c3 = (
            ld(ref) for ref in (rw_rw, ml_q, ml_k, ml_v, rt_q, rt_qx, rt_k, rt_v))
        y_rw = [_nt(rw_c[p], s_rw[p]) for p, _ in pairs]
        sg = [_dot(s_rw[p], rw_g[c, p]) for p, _ in pairs]
        qk_ml = [_nt(q_c[p], stack(k_c2[p])) for p, _ in pairs]
        qk_rt = [_nt(q_c3[p], stack(k_c3[p])) for p, _ in pairs]
        in_ml = [_dot(q_c[p], ml_c[c, p]) for p, _ in pairs]
        in_rt = [_dot(qx_c[p], rt_r[c, p]) for p, _ in pairs]

        gb = ml_gb[rows, :]
        gb_t = ml_gbt[c]
        sc_ml, floor, s_int = [], [], []
        for p, ps in pairs:
            h0, h1 = 2 * p, 2 * p + 1
            b_col = jnp.where(first, gb[:, N_HEADS + h0:N_HEADS + h0 + 1],
                              gb[:, N_HEADS + h1:N_HEADS + h1 + 1])
            b_row = jnp.concatenate([gb_t[N_HEADS + h0:N_HEADS + h0 + 1, :],
                                     gb_t[N_HEADS + h1:N_HEADS + h1 + 1, :]], axis=1)
            li_row = jnp.concatenate([gb_t[h0:h0 + 1, :], gb_t[h1:h1 + 1, :]], axis=1)
            d_log = jnp.where(tri_i, b_col - b_row + li_row, neg_inf)
            m_intra = jnp.where(first,
                                jnp.max(jnp.where(first, d_log, neg_inf), axis=-1, keepdims=True),
                                jnp.max(jnp.where(first, neg_inf, d_log), axis=-1, keepdims=True))
            m_inter = b_col + jnp.where(first, ml_sc[c, h0:h0 + 1, 0:1], ml_sc[c, h1:h1 + 1, 0:1])
            m_t = jnp.maximum(m_inter, m_intra)
            floor.append(jnp.exp(-m_t))
            s_int.append(jnp.exp(m_inter - m_t))
            sc_ml.append(qk_ml[p] * jnp.exp(d_log - m_t))
        sc_rt = [qk_rt[p] * dec[:, ps] for p, ps in pairs]
        tot_ml = [_dot(sc_ml[p], jnp.concatenate([stack(v_c2[p]).astype(BF16), ones_bd], axis=1))
                  for p, _ in pairs]
        intra_rt = [_dot(sc_rt[p], stack(v_c3[p])) for p, _ in pairs]
        for piece in ple_pieces[c * len(ple_pieces) // n_chunks:(c + 1) * len(ple_pieces) // n_chunks]:
            piece()
        for p, ps in pairs:
            tot = tot_ml[p] + jnp.concatenate([s_int[p], s_int[p]], axis=1) * in_ml[p]
            rw_y[rows, ps] = rw_y[rows, ps] + y_rw[p]
            ml_y[rows, ps] = tot[:, 0:PAIR] / jnp.maximum(jnp.abs(tot[:, PAIR:2 * PAIR]), floor[p])
            rt_y[rows, ps] = intra_rt[p] + in_rt[p]
            s_rw[p] = s_rw[p] * rw_dec[c, 0:1, ps] - sg[p] + rw_h[c, p]
    for p, _ in pairs:
        rw_state[p] = s_rw[p]

    y_a = (_head_norm(rw_y[...], seg_mean, RWKV_GN_EPS) * rw_ln_w + rw_ln_b + bonus)
    mixed[:, 0:GROUP] = y_a * mixed[:, 0:GROUP]

    sums = [ubuf[...]]
    for sh in (1, 2, 4, 8):
        sums.append(sums[-1] + pltpu.roll(sums[-1], sh, axis=0))
    pgrp = lax.broadcasted_iota(jnp.int32, (T, GROUP), 1) // (GROUP // len(POOL_WINDOWS))
    wsum = sums[1][POOL_HIST:, :]
    for gi in range(1, len(POOL_WINDOWS)):
        wsum = jnp.where(pgrp == gi, sums[gi + 1][POOL_HIST:, :], wsum)
    pos = (t_idx * T + lax.broadcasted_iota(jnp.int32, (T, 1), 0) + 1).astype(F32)
    win = jnp.exp2((pgrp[0:1, :] + 1).astype(F32))
    mean = wsum * jnp.where(pos >= win, 1.0 / win, 1.0 / pos)
    ubuf[0:POOL_HIST, :] = ubuf[T:T + POOL_HIST, :]
    y_b = _dot(mean - sums[0][POOL_HIST:, :], plw_ref[...]) * pl_scale
    mixed[:, GROUP:2 * GROUP] = y_b * mixed[:, GROUP:2 * GROUP]

    hcm = ml_y[...] * ml_og[...]
    y_c = _head_norm(hcm, seg_mean, HEAD_NORM_EPS) * mlln_ref[...]
    mixed[:, 2 * GROUP:3 * GROUP] = y_c * mixed[:, 2 * GROUP:3 * GROUP]

    y_d = _head_norm(rt_y[...], seg_mean, HEAD_NORM_EPS)
    mixed[:, 3 * GROUP:4 * GROUP] = y_d * mixed[:, 3 * GROUP:4 * GROUP]

    lnp = lnp_ref[...]
    for r0 in range(0, T, OUT_ROWS):
        rs = slice(r0, min(r0 + OUT_ROWS, T))
        hres = alpha * x_ref[rs, :] + _dot(mixed[rs, :], w_out_ref[...]) + o_ref[rs, :]
        mu_ln = jnp.mean(hres, axis=-1, keepdims=True)
        hc = hres - mu_ln
        var = jnp.mean(hc * hc, axis=-1, keepdims=True)
        o_ref[rs, :] = hc * lax.rsqrt(var + LN_EPS) * lnp[0:1, :] + lnp[1:2, :]


def _full(shape):
    return pl.BlockSpec(shape, lambda b, t: (0,) * len(shape))


def _per_layer(arr, layer):
    nd = arr.ndim - 1
    return pl.BlockSpec((None,) + arr.shape[1:], lambda b, t: (layer,) + (0,) * nd)


def _layer_call(x, p, layer_params, shared, *, layer, alpha, tile_t):
    B, S, _ = x.shape
    T = tile_t
    grid = (B, S // T)
    tile = lambda width: pl.BlockSpec((None, T, width), lambda b, t: (b, t, 0))
    in_specs = [tile(D_MODEL),
                pl.BlockSpec((None, None, T, D_PLE), lambda b, t: (layer, b, t, 0))]
    names = ["wr", "wa", "wb", "w_out", "ple_gate", "ple_w", "mu", "rwp", "lora", "plw", "conv",
             "gbias", "mlln", "lnp"]
    in_specs += [_per_layer(layer_params[n], layer) for n in names]
    in_specs += [pl.BlockSpec((T, 2 * PAIR_W), lambda b, t: (t, 0))]
    tail = ["dec", "rtp", "seg", "ltri"]
    in_specs += [_full(shared[n].shape) for n in tail]
    act = lambda width: pltpu.VMEM((T, width), F32)
    opnd = lambda width: pltpu.VMEM((T, width), BF16)
    n_chunks = T // CHUNK
    per_chunk = lambda r, w: pltpu.VMEM((n_chunks, N_PAIRS, r, w), F32)
    scratch = [
        pltpu.VMEM((T + HIST, RWKV_SHIFT), F32),
        pltpu.VMEM((T + HIST, 2 * GROUP), F32),
        pltpu.VMEM((T + POOL_HIST, GROUP), F32),
        act(GROUP), act(GROUP), opnd(GROUP), act(GROUP), act(GROUP), act(GROUP), act(GROUP),
        pltpu.VMEM((N_PAIRS, PAIR_W, PAIR_W), F32),
        act(GROUP),
        per_chunk(PAIR_W, PAIR_W), per_chunk(PAIR_W, PAIR_W),
        pltpu.VMEM((n_chunks, 8, GROUP), F32),
        opnd(GROUP), opnd(GROUP), opnd(GROUP), act(GATE_PAD), act(GROUP),
        pltpu.VMEM((N_PAIRS, PAIR_W, 2 * PAIR_W), F32),
        pltpu.VMEM((8, 128), F32),
        per_chunk(PAIR_W, 2 * PAIR_W),
        act(GATE_PAD),
        pltpu.VMEM((n_chunks, 8, CHUNK), F32),
        pltpu.VMEM((n_chunks, 8, 128), F32),
        act(GROUP),
        opnd(GROUP), opnd(GROUP), opnd(GROUP), opnd(GROUP), opnd(GROUP), act(GROUP),
        pltpu.VMEM((N_PAIRS, PAIR_W, PAIR_W), F32),
        per_chunk(PAIR_W, PAIR_W),
        act(D_MODEL),
        pltpu.VMEM((D_MODEL, N_COLS_PACKED - GATE_PAD - C_MLO), BF16),
    ]
    args = ([x, p] + [layer_params[n] for n in names] + [shared["cs"]]
            + [shared[n] for n in tail])
    return pl.pallas_call(
        functools.partial(_layer_kernel, alpha=alpha),
        grid=grid,
        in_specs=in_specs,
        out_specs=tile(D_MODEL),
        out_shape=jax.ShapeDtypeStruct((B, S, D_MODEL), F32),
        scratch_shapes=scratch,
        compiler_params=pltpu.CompilerParams(
            dimension_semantics=("arbitrary", "arbitrary"),
            vmem_limit_bytes=56 * 1024 * 1024),
        name="deepnorm_mixer_layer",
    )(*args)


def _block_diag(blocks):
    n = len(blocks)
    rows = []
    for i, blk in enumerate(blocks):
        z = jnp.zeros_like(blk)
        rows.append(jnp.concatenate([blk if j == i else z for j in range(n)], axis=-1))
    return jnp.concatenate(rows, axis=-2)


def kernel(x, p, w_in, rw_mu, rw_w0, rw_w2, rw_a0, rw_a2, rw_kk, rw_ka, rw_rk, rw_ln_w, rw_ln_b, pl_w, pl_scale, ml_conv, ml_ib, ml_fb, ml_ln_w, w_out, ple_w, ple_gate, ln_g, ln_b):
    B, S, D = x.shape
    depth = w_in.shape[0]
    alpha = (2.0 * depth) ** 0.25
    tile_t = min(TILE_T, S)
    assert D == D_MODEL and S % tile_t == 0 and tile_t % CHUNK == 0

    pos = jnp.arange(S, dtype=F32)
    inv_freq = ROPE_BASE ** (-jnp.arange(0, HEAD_DIM, 2, dtype=F32) / HEAD_DIM)
    ang = pos[:, None] * inv_freq[None, :]
    cos, sin = jnp.cos(ang), jnp.sin(ang)
    lane = np.arange(PAIR_W)
    expand = np.zeros((HEAD_DIM, 2 * PAIR_W), np.float32)
    expand[(lane % HEAD_DIM) // 2, lane] = 1.0
    expand[HEAD_DIM // 2 + (lane % HEAD_DIM) // 2, PAIR_W + lane] = np.where(lane % 2 == 0, -1.0, 1.0)
    cs = jnp.dot(jnp.concatenate([cos, sin], axis=1), jnp.asarray(expand), precision=lax.Precision.HIGHEST)
    log_g = jnp.log1p(-jnp.exp2(-5.0 - jnp.arange(N_HEADS, dtype=F32)))
    cpos = jnp.arange(CHUNK, dtype=F32)
    rel = cpos[:, None] - cpos[None, :]
    dec = jnp.where(rel >= 0, jnp.exp(log_g[:, None, None] * jnp.maximum(rel, 0.0)), 0.0)
    dec = dec.transpose(1, 0, 2).reshape(CHUNK, GROUP)
    zeta = jnp.exp(log_g[:, None] * (CHUNK - 1.0 - cpos))
    xi = jnp.exp(log_g[:, None] * (cpos + 1.0))
    g_chunk = jnp.exp(log_g * CHUNK)
    lanes = lambda t: jnp.repeat(t.T, HEAD_DIM, axis=1)
    rtp = jnp.concatenate([lanes(xi), lanes(zeta),
                           jnp.broadcast_to(jnp.repeat(g_chunk, HEAD_DIM)[None, :], (CHUNK, GROUP))], axis=1)
    head_of = np.arange(GROUP) // HEAD_DIM
    shared = {
        "cs": cs, "dec": dec, "rtp": rtp,
        "seg": jnp.asarray(head_of[:, None] == head_of[None, :], BF16),
        "ltri": jnp.asarray(np.tril(np.ones((CHUNK, CHUNK))), BF16),
    }

    g0 = C_MLO
    g1 = g0 + 2 * N_HEADS
    w_in_b = w_in.astype(BF16)
    layer_params = {
        "wr": jnp.concatenate([w_in_b[:, :, :C_RWG],
                               jnp.pad(w_in_b[:, :, g0:g1], ((0, 0), (0, 0), (0, GATE_PAD - 2 * N_HEADS)))], axis=2),
        "wa": w_in_b[:, :, C_RWG:g0],
        "wb": w_in_b[:, :, g0:],
        "w_out": w_out.astype(BF16),
        "ple_gate": ple_gate.astype(BF16),
        "ple_w": ple_w.astype(BF16),
        "mu": rw_mu[:, None, :].astype(F32),
        "rwp": jnp.stack([rw_w0, rw_a0, rw_kk, rw_ka, rw_rk, rw_ln_w, rw_ln_b, pl_scale], axis=1).astype(F32),
        "lora": _block_diag([rw_w2, rw_a2]).astype(BF16),
        "plw": _block_diag([pl_w[:, g] for g in range(len(POOL_WINDOWS))]).astype(BF16),
        "conv": jnp.pad(ml_conv.astype(F32), ((0, 0), (0, 8 - CONV_K), (0, 0))),
        "gbias": jnp.pad(jnp.concatenate([ml_ib, ml_fb], axis=1).astype(F32),
                         ((0, 0), (0, GATE_PAD - 2 * N_HEADS)))[:, None, :],
        "mlln": ml_ln_w[:, None, :].astype(F32),
        "lnp": jnp.stack([ln_g, ln_b], axis=1).astype(F32),
    }
    for i in range(depth):
        x = _layer_call(x, p, layer_params, shared, layer=i, alpha=alpha, tile_t=tile_t)
    return x
```

```python
import functools

import jax
import jax.numpy as jnp
import numpy as np
from jax import lax
from jax.experimental import pallas as pl
from jax.experimental.pallas import tpu as pltpu

D_MODEL = 1024
N_HEADS = 4
HEAD_DIM = 64
GROUP = 256
CHUNK = 64
PAIR_W = 2 * HEAD_DIM
N_PAIRS = GROUP // PAIR_W
LORA = 64
POOL_WINDOWS = (2, 4, 8, 16)
POOL_HIST = 16
CONV_K = 4
HIST = 8
D_PLE = 256
ROPE_BASE = 10000.0
LN_EPS = 1e-5
HEAD_NORM_EPS = 1e-6
RWKV_GN_EPS = 64e-5
RWKV_SHIFT = 3 * GROUP + 2 * LORA
GATE_PAD = 128

C_RW = 0
C_RWG = C_RW + RWKV_SHIFT
C_PLU = C_RWG + GROUP
C_PLG = C_PLU + GROUP
C_MLQK = C_PLG + GROUP
C_MLV = C_MLQK + 2 * GROUP
C_MLO = C_MLV + GROUP
C_MLG = C_MLO + GROUP
C_RTQ = C_MLG + GROUP
C_RTK = C_RTQ + GROUP
C_RTV = C_RTK + GROUP
C_RTG = C_RTV + GROUP

SUBLANES = 8
LANES = 128
V7X_VMEM_BYTES = 64 * 1024 * 1024
VMEM_LIMIT_BYTES = V7X_VMEM_BYTES // 8 * 7

TILE_T = 256
OUT_ROWS = 256
PLE_COLS = 256

F32 = jnp.float32
BF16 = jnp.bfloat16


def _dot(a, b):
    return jnp.dot(a.astype(BF16), b.astype(BF16), preferred_element_type=F32)


def _nt(a, b):
    return lax.dot_general(a.astype(BF16), b.astype(BF16), (((1,), (1,)), ((), ())),
                           preferred_element_type=F32)


def _tn(a, b):
    return lax.dot_general(a.astype(BF16), b.astype(BF16), (((0,), (0,)), ((), ())),
                           preferred_element_type=F32)


def _split3(x):
    hi = x.astype(BF16)
    r1 = x - hi.astype(F32)
    mid = r1.astype(BF16)
    lo = (r1 - mid.astype(F32)).astype(BF16)
    return hi, mid, lo


def _dot_exact_lhs(m, x):
    n = x.shape[1]
    r = jnp.dot(m, jnp.concatenate(_split3(x), axis=1), preferred_element_type=F32)
    return r[:, 0:n] + r[:, n:2 * n] + r[:, 2 * n:3 * n]


def _nt_exact_lhs(m, x):
    n = x.shape[0]
    r = lax.dot_general(m, jnp.concatenate(_split3(x), axis=0), (((1,), (1,)), ((), ())),
                        preferred_element_type=F32)
    return r[:, 0:n] + r[:, n:2 * n] + r[:, 2 * n:3 * n]


def _segsum(x, seg):
    return jnp.dot(x.astype(BF16), seg, preferred_element_type=F32)


def _head_norm(y, seg_mean, eps):
    yc = y - _segsum(y, seg_mean)
    return yc * lax.rsqrt(_segsum(yc * yc, seg_mean) + eps)


def _softplus(x):
    return jnp.maximum(x, 0.0) + jnp.log(1.0 + jnp.exp(-jnp.abs(x)))


def _sigmoid(x):
    return 0.5 * jnp.tanh(0.5 * x) + 0.5


def _silu(x):
    h = 0.5 * x
    return h * jnp.tanh(h) + h


def _layer_kernel(
        x_ref, p_ref, wr_ref, wa_ref, wb_ref, w_out_ref, ple_gate_ref, ple_w_ref,
        mu_ref, rwp_ref, lora_ref, plw_ref, conv_ref, gbias_ref, mlln_ref, lnp_ref,
        cst_ref, csr_ref, dec_ref, rtp_ref, seg_ref, ltri_ref,
        o_ref,
        zbuf, cbuf, ubuf,
        rw_r, rw_k, rw_v, rw_lw, rw_kk, rw_b, rw_y, rw_state,
        rw_rw, rw_g, rw_h, rw_dec,
        ml_q, ml_k, ml_v, ml_gt, ml_y, ml_state, ml_m,
        ml_c, ml_gb, ml_gbt, ml_sc, ml_og,
        rt_q, rt_qx, rt_k, rt_kz, rt_v, rt_y, rt_state,
        rt_r,
        mixed,
        *, alpha):
    NB, TB = x_ref.shape[0], x_ref.shape[1]
    T = NB * TB
    n_chunks = T // CHUNK
    chunks_per_seq = TB // CHUNK
    t_idx = pl.program_id(0)
    seq_rows = [slice(b * TB, (b + 1) * TB) for b in range(NB)]

    @pl.when(t_idx == 0)
    def _reset():
        zbuf[:, 0:HIST, :] = jnp.zeros((NB, HIST, RWKV_SHIFT), F32)
        cbuf[:, 0:HIST, :] = jnp.zeros((NB, HIST, 2 * GROUP), F32)
        ubuf[:, 0:POOL_HIST, :] = jnp.zeros((NB, POOL_HIST, GROUP), F32)
        rw_state[...] = jnp.zeros_like(rw_state)
        ml_state[...] = jnp.zeros_like(ml_state)
        ml_m[...] = jnp.zeros_like(ml_m)
        rt_state[...] = jnp.zeros_like(rt_state)

    def per_seq(fn):
        return jnp.concatenate([fn(b) for b in range(NB)], axis=0)

    xb = per_seq(lambda b: x_ref[b].astype(BF16))
    seg = seg_ref[...]
    seg_mean = (seg.astype(F32) * (1.0 / HEAD_DIM)).astype(BF16)
    ltri = ltri_ref[...]

    def proj(c0, width):
        if c0 < C_MLO:
            w = wa_ref[:, c0 - C_RWG:c0 - C_RWG + width]
        else:
            w = wb_ref[:, c0 - C_MLO:c0 - C_MLO + width]
        return jnp.dot(xb, w, preferred_element_type=F32)

    def prep_ml_qk():
        zqk = proj(C_MLQK, 2 * GROUP)
        cw = conv_ref[...]
        conv = zqk * cw[CONV_K - 1:CONV_K, :]
        for b in range(NB):
            cbuf[b, HIST:HIST + TB, :] = zqk[seq_rows[b], :]
        for j in range(1, CONV_K):
            conv = conv + per_seq(lambda b: cbuf[b, HIST - j:HIST - j + TB, :]) * cw[CONV_K - 1 - j:CONV_K - j, :]
        for b in range(NB):
            cbuf[b, 0:HIST, :] = cbuf[b, TB:TB + HIST, :]
        qk = _silu(conv)
        ml_q[...] = qk[:, 0:GROUP]
        ml_k[...] = qk[:, GROUP:2 * GROUP] * (HEAD_DIM ** -0.5)

    def prep_ml_v():
        ml_v[...] = proj(C_MLV, GROUP)
        zg = ml_gt[...] + gbias_ref[...]
        glane = lax.broadcasted_iota(jnp.int32, zg.shape, 1)
        ml_gt[...] = jnp.where(glane < N_HEADS, zg,
                               jnp.where(glane < 2 * N_HEADS, -_softplus(-zg), 0.0))

    rtp = rtp_ref[...]

    def rotary(z):
        ca, sa = cst_ref[:, 0:PAIR_W], cst_ref[:, PAIR_W:2 * PAIR_W]
        cos = ca * csr_ref[:, 0:PAIR_W] - sa * csr_ref[:, PAIR_W:2 * PAIR_W]
        sin = sa * csr_ref[:, 2 * PAIR_W:3 * PAIR_W] + ca * csr_ref[:, 3 * PAIR_W:4 * PAIR_W]
        cos, sin = per_seq(lambda b: cos), per_seq(lambda b: sin)
        even = lax.broadcasted_iota(jnp.int32, (T, GROUP), 1) % 2 == 0
        partner = jnp.where(even, pltpu.roll(z, GROUP - 1, axis=1), pltpu.roll(z, 1, axis=1))
        return z * jnp.concatenate([cos, cos], axis=1) + partner * jnp.concatenate([sin, sin], axis=1)

    def prep_rt(c0, dst, dst_scaled, scale, lo):
        def run():
            rot = rotary(proj(c0, GROUP)) * scale
            dst[...] = rot
            for c in range(n_chunks):
                rows = slice(c * CHUNK, (c + 1) * CHUNK)
                dst_scaled[rows, :] = rot[rows, :] * rtp[:, lo:lo + GROUP]
        return run

    def prep_rt_v():
        rt_v[...] = proj(C_RTV, GROUP)

    def gate_piece(c0, dst_ref, lo, act):
        def run():
            dst_ref[:, lo:lo + GROUP] = act(proj(c0, GROUP))
        return run

    def prep_pool_u():
        u = proj(C_PLU, GROUP)
        for b in range(NB):
            ubuf[b, POOL_HIST:POOL_HIST + TB, :] = u[seq_rows[b], :]

    def ple_piece(q):
        def run():
            cols = slice(q * PLE_COLS, (q + 1) * PLE_COLS)
            gate = _sigmoid(jnp.dot(xb, ple_gate_ref[:, cols], preferred_element_type=F32))
            emb = gate * _dot(per_seq(lambda b: p_ref[b]), ple_w_ref[:, cols])
            for b in range(NB):
                o_ref[b, :, cols] = emb[seq_rows[b], :]
        return run

    fillers = [prep_ml_qk, prep_ml_v,
               prep_rt(C_RTQ, rt_q, rt_qx, 1.0, 0), prep_rt(C_RTK, rt_k, rt_kz, HEAD_DIM ** -0.5, GROUP),
               prep_rt_v]
    n_late_fillers = 6
    fillers += [gate_piece(C_RWG, mixed, 0, _silu),
               prep_pool_u,
               gate_piece(C_PLG, mixed, GROUP, _silu),
               gate_piece(C_MLO, ml_og, 0, _sigmoid),
               gate_piece(C_MLG, mixed, 2 * GROUP, _silu),
               gate_piece(C_RTG, mixed, 3 * GROUP, _silu)]
    ple_pieces = [ple_piece(q) for q in range(D_MODEL // PLE_COLS)]

    def fill(n=1):
        for _ in range(min(n, len(fillers))):
            fillers.pop(0)()

    z1 = jnp.dot(xb, wr_ref[...], preferred_element_type=F32)
    zrw = z1[:, 0:RWKV_SHIFT]
    ml_gt[...] = z1[:, RWKV_SHIFT:RWKV_SHIFT + GATE_PAD]
    fill(2)
    for b in range(NB):
        zbuf[b, HIST:HIST + TB, :] = zrw[seq_rows[b], :]
    prev = per_seq(lambda b: zbuf[b, HIST - 1:HIST - 1 + TB, :])
    zs = zrw + (prev - zrw) * mu_ref[...]
    for b in range(NB):
        zbuf[b, 0:HIST, :] = zbuf[b, TB:TB + HIST, :]
    r = zs[:, 0:GROUP]
    k = zs[:, GROUP:2 * GROUP]
    v = zs[:, 2 * GROUP:3 * GROUP]
    wa = zs[:, 3 * GROUP:RWKV_SHIFT]
    lane = lax.broadcasted_iota(jnp.int32, wa.shape, 1)
    wa = jnp.where(lane < LORA, jnp.tanh(wa), wa)
    lo = _dot(wa, lora_ref[...])
    fill()
    rwp = rwp_ref[...]
    w0, a0, k_k, k_a, r_k, rw_ln_w, rw_ln_b, pl_scale = (rwp[i:i + 1, :] for i in range(8))
    w_log = -_softplus(-(w0 + lo[:, 0:GROUP])) - 0.5
    a = _sigmoid(a0 + lo[:, GROUP:2 * GROUP])
    kk = k * k_k
    kk = kk * lax.rsqrt(jnp.maximum(_segsum(kk * kk, seg), 1e-24))
    k2 = k * (1.0 + (a - 1.0) * k_a)
    rw_r[...] = r
    rw_k[...] = k2
    rw_v[...] = v
    rw_lw[...] = -jnp.exp(w_log)
    rw_kk[...] = kk
    rw_b[...] = kk * a
    bonus = _segsum(r * k2 * r_k, seg) * v

    PAIR = PAIR_W
    ri = lax.broadcasted_iota(jnp.int32, (PAIR, PAIR), 0)
    ci = lax.broadcasted_iota(jnp.int32, (PAIR, PAIR), 1)
    same_blk = (ri // HEAD_DIM) == (ci // HEAD_DIM)
    ti = lax.broadcasted_iota(jnp.int32, (CHUNK, PAIR), 0)
    si = lax.broadcasted_iota(jnp.int32, (CHUNK, PAIR), 1) % HEAD_DIM
    tri_s = ti > si
    tri_i = ti >= si
    eye = jnp.where(ti == si, 1.0, 0.0).astype(F32)

    def merge_mask(b):
        return ((ti // (2 * b)) == (si // (2 * b))) & ((ti // b) != (si // b))
    first = lax.broadcasted_iota(jnp.int32, (CHUNK, PAIR), 1) < HEAD_DIM
    first_row = lax.broadcasted_iota(jnp.int32, (1, 2 * PAIR), 1) % PAIR < HEAD_DIM
    e8 = jnp.where(lax.broadcasted_iota(jnp.int32, (SUBLANES, GATE_PAD), 0)
                   == lax.broadcasted_iota(jnp.int32, (SUBLANES, GATE_PAD), 1), 1.0, 0.0).astype(BF16)
    ones_bd = jnp.where(same_blk, 1.0, 0.0).astype(BF16)
    dec = dec_ref[...]
    g_chunk = rtp[0:1, 2 * GROUP:3 * GROUP]
    pairs = [(p, slice(p * PAIR, (p + 1) * PAIR)) for p in range(N_PAIRS)]

    def rows_of(c):
        if isinstance(c, int):
            return slice(c * CHUNK, (c + 1) * CHUNK)
        return pl.ds(pl.multiple_of(c * CHUNK, CHUNK), CHUNK)

    def dup(x):
        return jnp.concatenate([x, x], axis=0)

    def stack(x):
        return jnp.where(same_blk, dup(x), 0.0)

    def stack2(x):
        return jnp.concatenate([stack(x[:, 0:PAIR]), stack(x[:, PAIR:2 * PAIR])], axis=1)


    def state_scans():
        for b in range(NB):
            s_ml = [ml_state[b, p] for p, _ in pairs]
            m_ml = [ml_m[b, h:h + 1, 0:1] for h in range(N_HEADS)]
            s_rt = [rt_state[b, p] for p, _ in pairs]
            for c in range(b * chunks_per_seq, (b + 1) * chunks_per_seq):
                for p, ps in pairs:
                    s_old, s_new = [], []
                    for h in (2 * p, 2 * p + 1):
                        m_chunk = ml_sc[c, h:h + 1, 0:1]
                        b_last = ml_sc[c, N_HEADS + h:N_HEADS + h + 1, 0:1]
                        m_new = jnp.maximum(b_last + m_ml[h], m_chunk)
                        s_old.append(jnp.exp(b_last + m_ml[h] - m_new))
                        s_new.append(jnp.exp(m_chunk - m_new))
                        ml_sc[c, h:h + 1, :] = jnp.broadcast_to(m_ml[h], (1, LANES))
                        m_ml[h] = m_new
                    inc = ml_c[c, p]
                    ml_c[c, p] = s_ml[p]
                    s_ml[p] = (jnp.where(first_row, s_old[0], s_old[1]) * s_ml[p]
                               + jnp.where(first_row, s_new[0], s_new[1]) * inc)

                    inc = rt_r[c, p]
                    rt_r[c, p] = s_rt[p]
                    s_rt[p] = g_chunk[:, ps] * s_rt[p] + inc
            for p, _ in pairs:
                ml_state[b, p] = s_ml[p]
                rt_state[b, p] = s_rt[p]
            for h in range(N_HEADS):
                ml_m[b, h:h + 1, :] = jnp.broadcast_to(m_ml[h], (1, LANES))

    def pass1():
        cs = list(range(n_chunks))
        chains = [(j, p, ps) for j in range(n_chunks) for p, ps in pairs]

        prep = []
        for c in cs:
            rows = rows_of(c)
            lw = rw_lw[rows, :]
            cum = _dot_exact_lhs(ltri, lw)
            g_l = cum[CHUNK - 1:CHUNK, :]
            e_neg = jnp.exp(-cum)
            kk_c = rw_kk[rows, :]
            b_c = rw_b[rows, :]
            k_c = rw_k[rows, :]
            e_g = jnp.exp(g_l - cum)
            rw_dec[c] = jnp.broadcast_to(jnp.exp(g_l), (SUBLANES, GROUP))
            prep.append(dict(al=jnp.exp(cum - lw) * kk_c, be=b_c * e_neg, kt=k_c * e_neg,
                             rt=rw_r[rows, :] * jnp.exp(cum), kh=k_c * e_g, bh=b_c * e_g,
                             v=rw_v[rows, :]))
        pick = lambda name: [prep[j][name][:, ps] for j, p, ps in chains]
        al, be, kt, rt, kh, bh, vv = (pick(n) for n in ("al", "be", "kt", "rt", "kh", "bh", "v"))
        ar = [jnp.concatenate([a, r], axis=0) for a, r in zip(al, rt)]
        nrpq = [_nt(x, jnp.concatenate([stack(y), stack(z)], axis=0))
                for x, y, z in zip(ar, be, kt)]
        fill(2)
        n_m = [jnp.where(tri_s, t[0:CHUNK, 0:PAIR], 0.0) for t in nrpq]
        rb_m = [jnp.where(tri_i, t[CHUNK:2 * CHUNK, 0:PAIR], 0.0) for t in nrpq]
        p_m = [jnp.where(tri_s, t[0:CHUNK, PAIR:2 * PAIR], 0.0) for t in nrpq]
        q_m = [jnp.where(tri_i, t[CHUNK:2 * CHUNK, PAIR:2 * PAIR], 0.0) for t in nrpq]
        pqv = [_dot(jnp.concatenate([x, y], axis=0), stack(v)) for x, y, v in zip(p_m, q_m, vv)]
        fill()
        tinv = [eye - jnp.where(merge_mask(1), n, 0.0) for n in n_m]

        def merge_step(b, tinv):
            mask = merge_mask(b)
            dc = [_dot(t, stack(jnp.where(mask, n, 0.0))) for t, n in zip(tinv, n_m)]
            fill()
            out = [t - _dot(x, stack(t)) for t, x in zip(tinv, dc)]
            fill()
            return out

        tinv = merge_step(2, tinv)

        while len(fillers) > n_late_fillers:
            fill()
        late = []
        for c in cs:
            rows = rows_of(c)
            gt = ml_gt[rows, :]
            bcum = _dot_exact_lhs(ltri, gt)
            gl2 = lax.broadcasted_iota(jnp.int32, gt.shape, 1)
            gb = jnp.where(gl2 < N_HEADS, gt, bcum)
            gb_t = _nt_exact_lhs(e8, gb)
            ml_gb[rows, :] = gb
            ml_gbt[c] = gb_t
            b_last4 = gb_t[N_HEADS:2 * N_HEADS, CHUNK - 1:CHUNK]
            m_chunk4 = jnp.max(b_last4 - gb_t[N_HEADS:2 * N_HEADS, :] + gb_t[0:N_HEADS, :],
                               axis=-1, keepdims=True)
            ml_sc[c] = jnp.concatenate([jnp.broadcast_to(m_chunk4, (N_HEADS, LANES)),
                                        jnp.broadcast_to(b_last4, (N_HEADS, LANES))], axis=0)
            k_c2 = ml_k[rows, :]
            v_c2 = ml_v[rows, :]
            kz_c = rt_kz[rows, :]
            v_c3 = rt_v[rows, :]
            for p, ps in pairs:
                wg = []
                for h in (2 * p, 2 * p + 1):
                    b_col = gb[:, N_HEADS + h:N_HEADS + h + 1]
                    li_col = gb[:, h:h + 1]
                    wg.append(jnp.exp(b_last4[h:h + 1] - b_col + li_col - m_chunk4[h:h + 1]))
                wgt = jnp.where(first, wg[0], wg[1])
                late.append((c, p, k_c2[:, ps], jnp.concatenate([v_c2[:, ps] * wgt, wgt], axis=1),
                             kz_c[:, ps], v_c3[:, ps]))

        for b in (4, 8):
            tinv = merge_step(b, tinv)
        for c, p, k_p, vw, kz_p, v_p in late:
            cc = _tn(k_p, vw)
            ml_c[c, p] = jnp.concatenate([jnp.where(same_blk, cc[:, 0:PAIR], 0.0),
                                          jnp.where(same_blk, cc[:, PAIR:2 * PAIR], 0.0)], axis=1)
            rt_r[c, p] = jnp.where(same_blk, _tn(kz_p, v_p), 0.0)
        state_scans()
        for b in (16, 32):
            tinv = merge_step(b, tinv)
        wu = [_dot(t, jnp.concatenate([stack(a), stack(x[0:CHUNK])], axis=1))
              for t, a, x in zip(tinv, al, pqv)]
        fill()
        rbwu = [_dot(x, stack2(y)) for x, y in zip(rb_m, wu)]
        gu = [_tn(y, x) for y, x in zip(wu, bh)]
        fill()
        vk = [_tn(x, y) for x, y in zip(vv, kh)]
        for n, (j, p, ps) in enumerate(chains):
            rw_g[cs[j], p] = jnp.where(same_blk, gu[n][0:PAIR], 0.0)
            rw_h[cs[j], p] = jnp.where(same_blk, vk[n] - gu[n][PAIR:2 * PAIR], 0.0)
            rows = rows_of(cs[j])
            rw_rw[rows, ps] = rt[n] - rbwu[n][:, 0:PAIR]
            rw_y[rows, ps] = pqv[n][CHUNK:2 * CHUNK] - rbwu[n][:, PAIR:2 * PAIR]

    pass1()
    while fillers:
        fill()

    neg_inf = jnp.full((CHUNK, PAIR), -jnp.inf, F32)
    s_rw_all = [[rw_state[b, p] for p, _ in pairs] for b in range(NB)]
    order = [(b, b * chunks_per_seq + j) for j in range(chunks_per_seq) for b in range(NB)]
    for n_done, (b, c) in enumerate(order):
        s_rw = s_rw_all[b]
        rows = rows_of(c)
        ld = lambda ref: [ref[rows, ps] for p, ps in pairs]
        rw_c, q_c, k_c2, v_c2, q_c3, qx_c, k_c3, v_c3 = (
            ld(ref) for ref in (rw_rw, ml_q, ml_k, ml_v, rt_q, rt_qx, rt_k, rt_v))
        y_rw = [_nt(rw_c[p], s_rw[p]) for p, _ in pairs]
        sg = [_dot(s_rw[p], rw_g[c, p]) for p, _ in pairs]
        qk_ml = [_nt(q_c[p], stack(k_c2[p])) for p, _ in pairs]
        qk_rt = [_nt(q_c3[p], stack(k_c3[p])) for p, _ in pairs]
        in_ml = [_dot(q_c[p], ml_c[c, p]) for p, _ in pairs]
        in_rt = [_dot(qx_c[p], rt_r[c, p]) for p, _ in pairs]

        gb = ml_gb[rows, :]
        gb_t = ml_gbt[c]
        sc_ml, floor, s_int = [], [], []
        for p, ps in pairs:
            h0, h1 = 2 * p, 2 * p + 1
            b_col = jnp.where(first, gb[:, N_HEADS + h0:N_HEADS + h0 + 1],
                              gb[:, N_HEADS + h1:N_HEADS + h1 + 1])
            b_row = jnp.concatenate([gb_t[N_HEADS + h0:N_HEADS + h0 + 1, :],
                                     gb_t[N_HEADS + h1:N_HEADS + h1 + 1, :]], axis=1)
            li_row = jnp.concatenate([gb_t[h0:h0 + 1, :], gb_t[h1:h1 + 1, :]], axis=1)
            d_log = jnp.where(tri_i, b_col - b_row + li_row, neg_inf)
            m_intra = jnp.where(first,
                                jnp.max(jnp.where(first, d_log, neg_inf), axis=-1, keepdims=True),
                                jnp.max(jnp.where(first, neg_inf, d_log), axis=-1, keepdims=True))
            m_inter = b_col + jnp.where(first, ml_sc[c, h0:h0 + 1, 0:1], ml_sc[c, h1:h1 + 1, 0:1])
            m_t = jnp.maximum(m_inter, m_intra)
            floor.append(jnp.exp(-m_t))
            s_int.append(jnp.exp(m_inter - m_t))
            sc_ml.append(qk_ml[p] * jnp.exp(d_log - m_t))
        sc_rt = [qk_rt[p] * dec[:, ps] for p, ps in pairs]
        tot_ml = [_dot(sc_ml[p], jnp.concatenate([stack(v_c2[p]).astype(BF16), ones_bd], axis=1))
                  for p, _ in pairs]
        intra_rt = [_dot(sc_rt[p], stack(v_c3[p])) for p, _ in pairs]
        for piece in ple_pieces[n_done * len(ple_pieces) // n_chunks:(n_done + 1) * len(ple_pieces) // n_chunks]:
            piece()
        for p, ps in pairs:
            tot = tot_ml[p] + jnp.concatenate([s_int[p], s_int[p]], axis=1) * in_ml[p]
            rw_y[rows, ps] = rw_y[rows, ps] + y_rw[p]
            ml_y[rows, ps] = tot[:, 0:PAIR] / jnp.maximum(jnp.abs(tot[:, PAIR:2 * PAIR]), floor[p])
            rt_y[rows, ps] = intra_rt[p] + in_rt[p]
            s_rw[p] = s_rw[p] * rw_dec[c, 0:1, ps] - sg[p] + rw_h[c, p]
    for b in range(NB):
        for p, _ in pairs:
            rw_state[b, p] = s_rw_all[b][p]

    y_a = (_head_norm(rw_y[...], seg_mean, RWKV_GN_EPS) * rw_ln_w + rw_ln_b + bonus)
    mixed[:, 0:GROUP] = y_a * mixed[:, 0:GROUP]

    pgrp = lax.broadcasted_iota(jnp.int32, (TB, GROUP), 1) // (GROUP // len(POOL_WINDOWS))

    def window_sums(b):
        sums = [ubuf[b]]
        for sh in (1, 2, 4, 8):
            sums.append(sums[-1] + pltpu.roll(sums[-1], sh, axis=0))
        wsum = sums[1][POOL_HIST:, :]
        for gi in range(1, len(POOL_WINDOWS)):
            wsum = jnp.where(pgrp == gi, sums[gi + 1][POOL_HIST:, :], wsum)
        return wsum

    wsum = per_seq(window_sums)
    u_all = per_seq(lambda b: ubuf[b, POOL_HIST:POOL_HIST + TB, :])
    pos = per_seq(lambda b: (t_idx * TB + lax.broadcasted_iota(jnp.int32, (TB, 1), 0) + 1).astype(F32))
    win = jnp.exp2((pgrp[0:1, :] + 1).astype(F32))
    mean = wsum * jnp.where(pos >= win, 1.0 / win, 1.0 / pos)
    for b in range(NB):
        ubuf[b, 0:POOL_HIST, :] = ubuf[b, TB:TB + POOL_HIST, :]
    y_b = _dot(mean - u_all, plw_ref[...]) * pl_scale
    mixed[:, GROUP:2 * GROUP] = y_b * mixed[:, GROUP:2 * GROUP]

    hcm = ml_y[...] * ml_og[...]
    y_c = _head_norm(hcm, seg_mean, HEAD_NORM_EPS) * mlln_ref[...]
    mixed[:, 2 * GROUP:3 * GROUP] = y_c * mixed[:, 2 * GROUP:3 * GROUP]

    y_d = _head_norm(rt_y[...], seg_mean, HEAD_NORM_EPS)
    mixed[:, 3 * GROUP:4 * GROUP] = y_d * mixed[:, 3 * GROUP:4 * GROUP]

    lnp = lnp_ref[...]
    for b in range(NB):
        for r0 in range(0, TB, OUT_ROWS):
            rs = slice(r0, min(r0 + OUT_ROWS, TB))
            ms = slice(b * TB + rs.start, b * TB + rs.stop)
            hres = alpha * x_ref[b, rs, :] + _dot(mixed[ms, :], w_out_ref[...]) + o_ref[b, rs, :]
            mu_ln = jnp.mean(hres, axis=-1, keepdims=True)
            hc = hres - mu_ln
            var = jnp.mean(hc * hc, axis=-1, keepdims=True)
            o_ref[b, rs, :] = hc * lax.rsqrt(var + LN_EPS) * lnp[0:1, :] + lnp[1:2, :]


def _full(shape):
    return pl.BlockSpec(shape, lambda t: (0,) * len(shape))


def _per_layer(arr, layer):
    nd = arr.ndim - 1
    return pl.BlockSpec((None,) + arr.shape[1:], lambda t: (layer,) + (0,) * nd)


def _layer_call(x, p, layer_params, shared, *, layer, alpha, tile_t):
    B, S, _ = x.shape
    TB = tile_t
    T = B * TB
    grid = (S // TB,)
    tile = lambda width: pl.BlockSpec((B, TB, width), lambda t: (0, t, 0))
    in_specs = [tile(D_MODEL),
                pl.BlockSpec((None, B, TB, D_PLE), lambda t: (layer, 0, t, 0))]
    names = ["wr", "wa", "wb", "w_out", "ple_gate", "ple_w", "mu", "rwp", "lora", "plw", "conv",
             "gbias", "mlln", "lnp"]
    in_specs += [_per_layer(layer_params[n], layer) for n in names]
    in_specs += [pl.BlockSpec((None, 1, 2 * PAIR_W), lambda t: (t, 0, 0)),
                 _full(shared["csr"].shape)]
    tail = ["dec", "rtp", "seg", "ltri"]
    in_specs += [_full(shared[n].shape) for n in tail]
    act = lambda width: pltpu.VMEM((T, width), F32)
    n_chunks = T // CHUNK
    per_chunk = lambda r, w: pltpu.VMEM((n_chunks, N_PAIRS, r, w), F32)
    scratch = [
        pltpu.VMEM((B, TB + HIST, RWKV_SHIFT), F32),
        pltpu.VMEM((B, TB + HIST, 2 * GROUP), F32),
        pltpu.VMEM((B, TB + POOL_HIST, GROUP), F32),
        act(GROUP), act(GROUP), act(GROUP), act(GROUP), act(GROUP), act(GROUP), act(GROUP),
        pltpu.VMEM((B, N_PAIRS, PAIR_W, PAIR_W), F32),
        act(GROUP),
        per_chunk(PAIR_W, PAIR_W), per_chunk(PAIR_W, PAIR_W),
        pltpu.VMEM((n_chunks, SUBLANES, GROUP), F32),
        act(GROUP), act(GROUP), act(GROUP), act(GATE_PAD), act(GROUP),
        pltpu.VMEM((B, N_PAIRS, PAIR_W, 2 * PAIR_W), F32),
        pltpu.VMEM((B, SUBLANES, LANES), F32),
        per_chunk(PAIR_W, 2 * PAIR_W),
        act(GATE_PAD),
        pltpu.VMEM((n_chunks, SUBLANES, CHUNK), F32),
        pltpu.VMEM((n_chunks, SUBLANES, LANES), F32),
        act(GROUP),
        act(GROUP), act(GROUP), act(GROUP), act(GROUP), act(GROUP), act(GROUP),
        pltpu.VMEM((B, N_PAIRS, PAIR_W, PAIR_W), F32),
        per_chunk(PAIR_W, PAIR_W),
        act(D_MODEL),
    ]
    args = ([x, p] + [layer_params[n] for n in names] + [shared["cst"], shared["csr"]]
            + [shared[n] for n in tail])
    return pl.pallas_call(
        functools.partial(_layer_kernel, alpha=alpha),
        grid=grid,
        in_specs=in_specs,
        out_specs=tile(D_MODEL),
        out_shape=jax.ShapeDtypeStruct((B, S, D_MODEL), F32),
        scratch_shapes=scratch,
        compiler_params=pltpu.CompilerParams(
            dimension_semantics=("arbitrary",),
            vmem_limit_bytes=VMEM_LIMIT_BYTES),
        name="deepnorm_mixer_layer",
    )(*args)


def _block_diag(blocks):
    n = len(blocks)
    rows = []
    for i, blk in enumerate(blocks):
        z = jnp.zeros_like(blk)
        rows.append(jnp.concatenate([blk if j == i else z for j in range(n)], axis=-1))
    return jnp.concatenate(rows, axis=-2)


def kernel(x, p, w_in, rw_mu, rw_w0, rw_w2, rw_a0, rw_a2, rw_kk, rw_ka, rw_rk, rw_ln_w, rw_ln_b, pl_w, pl_scale, ml_conv, ml_ib, ml_fb, ml_ln_w, w_out, ple_w, ple_gate, ln_g, ln_b):
    B, S, D = x.shape
    depth = w_in.shape[0]
    alpha = (2.0 * depth) ** 0.25
    tile_t = min(TILE_T, S)
    assert D == D_MODEL and S % tile_t == 0 and tile_t % CHUNK == 0

    inv_freq = ROPE_BASE ** (-jnp.arange(0, HEAD_DIM, 2, dtype=F32) / HEAD_DIM)
    freq = jnp.tile(jnp.repeat(inv_freq, 2), 2)
    sgn = jnp.tile(jnp.asarray([-1.0, 1.0], F32), HEAD_DIM)
    ang_t = (jnp.arange(S // tile_t, dtype=F32) * tile_t)[:, None] * freq
    ang_r = jnp.arange(tile_t, dtype=F32)[:, None] * freq
    cst = jnp.concatenate([jnp.cos(ang_t), jnp.sin(ang_t)], axis=1)[:, None, :]
    csr = jnp.concatenate([jnp.cos(ang_r), jnp.sin(ang_r), jnp.cos(ang_r) * sgn, jnp.sin(ang_r) * sgn], axis=1)
    log_g = jnp.log1p(-jnp.exp2(-5.0 - jnp.arange(N_HEADS, dtype=F32)))
    cpos = jnp.arange(CHUNK, dtype=F32)
    rel = cpos[:, None] - cpos[None, :]
    dec = jnp.where(rel >= 0, jnp.exp(log_g[:, None, None] * jnp.maximum(rel, 0.0)), 0.0)
    dec = dec.transpose(1, 0, 2).reshape(CHUNK, GROUP)
    zeta = jnp.exp(log_g[:, None] * (CHUNK - 1.0 - cpos))
    xi = jnp.exp(log_g[:, None] * (cpos + 1.0))
    g_chunk = jnp.exp(log_g * CHUNK)
    lanes = lambda t: jnp.repeat(t.T, HEAD_DIM, axis=1)
    rtp = jnp.concatenate([lanes(xi), lanes(zeta),
                           jnp.broadcast_to(jnp.repeat(g_chunk, HEAD_DIM)[None, :], (CHUNK, GROUP))], axis=1)
    head_of = np.arange(GROUP) // HEAD_DIM
    shared = {
        "cst": cst, "csr": csr, "dec": dec, "rtp": rtp,
        "seg": jnp.asarray(head_of[:, None] == head_of[None, :], BF16),
        "ltri": jnp.asarray(np.tril(np.ones((CHUNK, CHUNK))), BF16),
    }

    g0 = C_MLO
    g1 = g0 + 2 * N_HEADS
    w_in_b = w_in.astype(BF16)
    layer_params = {
        "wr": jnp.concatenate([w_in_b[:, :, :C_RWG],
                               jnp.pad(w_in_b[:, :, g0:g1], ((0, 0), (0, 0), (0, GATE_PAD - 2 * N_HEADS)))], axis=2),
        "wa": w_in_b[:, :, C_RWG:g0],
        "wb": w_in_b[:, :, g1:],
        "w_out": w_out.astype(BF16),
        "ple_gate": ple_gate.astype(BF16),
        "ple_w": ple_w.astype(BF16),
        "mu": rw_mu[:, None, :].astype(F32),
        "rwp": jnp.stack([rw_w0, rw_a0, rw_kk, rw_ka, rw_rk, rw_ln_w, rw_ln_b, pl_scale], axis=1).astype(F32),
        "lora": _block_diag([rw_w2, rw_a2]).astype(BF16),
        "plw": _block_diag([pl_w[:, g] for g in range(len(POOL_WINDOWS))]).astype(BF16),
        "conv": jnp.pad(ml_conv.astype(F32), ((0, 0), (0, 8 - CONV_K), (0, 0))),
        "gbias": jnp.pad(jnp.concatenate([ml_ib, ml_fb], axis=1).astype(F32),
                         ((0, 0), (0, GATE_PAD - 2 * N_HEADS)))[:, None, :],
        "mlln": ml_ln_w[:, None, :].astype(F32),
        "lnp": jnp.stack([ln_g, ln_b], axis=1).astype(F32),
    }
    for i in range(depth):
        x = _layer_call(x, p, layer_params, shared, layer=i, alpha=alpha, tile_t=tile_t)
    return x
```

```python
import functools

import jax
import jax.numpy as jnp
import numpy as np
from jax import lax
from jax.experimental import pallas as pl
from jax.experimental.pallas import tpu as pltpu

D_MODEL = 1024
N_HEADS = 4
HEAD_DIM = 64
GROUP = 256
CHUNK = 64
PAIR_W = 2 * HEAD_DIM
N_PAIRS = GROUP // PAIR_W
LORA = 64
POOL_WINDOWS = (2, 4, 8, 16)
POOL_HIST = 16
CONV_K = 4
HIST = 8
D_PLE = 256
ROPE_BASE = 10000.0
LN_EPS = 1e-5
HEAD_NORM_EPS = 1e-6
RWKV_GN_EPS = 64e-5
RWKV_SHIFT = 3 * GROUP + 2 * LORA
GATE_PAD = 128

C_RW = 0
C_RWG = C_RW + RWKV_SHIFT
C_PLU = C_RWG + GROUP
C_PLG = C_PLU + GROUP
C_MLQK = C_PLG + GROUP
C_MLV = C_MLQK + 2 * GROUP
C_MLO = C_MLV + GROUP
C_MLG = C_MLO + GROUP
C_RTQ = C_MLG + GROUP
C_RTK = C_RTQ + GROUP
C_RTV = C_RTK + GROUP
C_RTG = C_RTV + GROUP

SUBLANES = 8
LANES = 128
V7X_VMEM_BYTES = 64 * 1024 * 1024
VMEM_LIMIT_BYTES = V7X_VMEM_BYTES // 8 * 7

TILE_T = 256
OUT_ROWS = 256
PLE_COLS = 256

F32 = jnp.float32
BF16 = jnp.bfloat16


def _dot(a, b):
    return jnp.dot(a.astype(BF16), b.astype(BF16), preferred_element_type=F32)


def _nt(a, b):
    return lax.dot_general(a.astype(BF16), b.astype(BF16), (((1,), (1,)), ((), ())),
                           preferred_element_type=F32)


def _tn(a, b):
    return lax.dot_general(a.astype(BF16), b.astype(BF16), (((0,), (0,)), ((), ())),
                           preferred_element_type=F32)


def _split3(x):
    hi = x.astype(BF16)
    r1 = x - hi.astype(F32)
    mid = r1.astype(BF16)
    lo = (r1 - mid.astype(F32)).astype(BF16)
    return hi, mid, lo


def _dot_exact_lhs(m, x):
    n = x.shape[1]
    r = jnp.dot(m, jnp.concatenate(_split3(x), axis=1), preferred_element_type=F32)
    return r[:, 0:n] + r[:, n:2 * n] + r[:, 2 * n:3 * n]


def _nt_exact_lhs(m, x):
    n = x.shape[0]
    r = lax.dot_general(m, jnp.concatenate(_split3(x), axis=0), (((1,), (1,)), ((), ())),
                        preferred_element_type=F32)
    return r[:, 0:n] + r[:, n:2 * n] + r[:, 2 * n:3 * n]


def _segsum(x, seg):
    return jnp.dot(x.astype(BF16), seg, preferred_element_type=F32)


def _head_norm(y, seg_mean, eps):
    yc = y - _segsum(y, seg_mean)
    return yc * lax.rsqrt(_segsum(yc * yc, seg_mean) + eps)


def _softplus(x):
    return jnp.maximum(x, 0.0) + jnp.log(1.0 + jnp.exp(-jnp.abs(x)))


def _sigmoid(x):
    return 0.5 * jnp.tanh(0.5 * x) + 0.5


def _silu(x):
    h = 0.5 * x
    return h * jnp.tanh(h) + h


def _layer_kernel(
        x_ref, p_ref, wr_ref, wa_ref, wb_ref, w_out_ref, ple_gate_ref, ple_w_ref,
        mu_ref, rwp_ref, lora_ref, plw_ref, conv_ref, gbias_ref, mlln_ref, lnp_ref,
        cst_ref, csr_ref, dec_ref, rtp_ref, seg_ref, ltri_ref,
        o_ref,
        zbuf, cbuf, ubuf,
        rw_r, rw_k, rw_v, rw_lw, rw_kk, rw_b, rw_y, rw_state,
        rw_rw, rw_g, rw_h, rw_dec,
        ml_q, ml_k, ml_v, ml_gt, ml_y, ml_state, ml_m,
        ml_c, ml_gb, ml_gbt, ml_sc, ml_og,
        rt_q, rt_qx, rt_k, rt_kz, rt_v, rt_y, rt_state,
        rt_r,
        mixed,
        *, alpha):
    NB, TB = x_ref.shape[0], x_ref.shape[1]
    T = NB * TB
    n_chunks = T // CHUNK
    chunks_per_seq = TB // CHUNK
    t_idx = pl.program_id(0)
    seq_rows = [slice(b * TB, (b + 1) * TB) for b in range(NB)]

    @pl.when(t_idx == 0)
    def _reset():
        zbuf[:, 0:HIST, :] = jnp.zeros((NB, HIST, RWKV_SHIFT), F32)
        cbuf[:, 0:HIST, :] = jnp.zeros((NB, HIST, 2 * GROUP), F32)
        ubuf[:, 0:POOL_HIST, :] = jnp.zeros((NB, POOL_HIST, GROUP), F32)
        rw_state[...] = jnp.zeros_like(rw_state)
        ml_state[...] = jnp.zeros_like(ml_state)
        ml_m[...] = jnp.zeros_like(ml_m)
        rt_state[...] = jnp.zeros_like(rt_state)

    def per_seq(fn):
        return jnp.concatenate([fn(b) for b in range(NB)], axis=0)

    xb = per_seq(lambda b: x_ref[b].astype(BF16))
    seg = seg_ref[...]
    seg_mean = (seg.astype(F32) * (1.0 / HEAD_DIM)).astype(BF16)
    ltri = ltri_ref[...]

    def proj(c0, width):
        if c0 < C_MLO:
            w = wa_ref[:, c0 - C_RWG:c0 - C_RWG + width]
        else:
            w = wb_ref[:, c0 - C_MLO:c0 - C_MLO + width]
        return jnp.dot(xb, w, preferred_element_type=F32)

    def prep_ml_qk():
        zqk = proj(C_MLQK, 2 * GROUP)
        cw = conv_ref[...]
        conv = zqk * cw[CONV_K - 1:CONV_K, :]
        for b in range(NB):
            cbuf[b, HIST:HIST + TB, :] = zqk[seq_rows[b], :]
        for j in range(1, CONV_K):
            conv = conv + per_seq(lambda b: cbuf[b, HIST - j:HIST - j + TB, :]) * cw[CONV_K - 1 - j:CONV_K - j, :]
        for b in range(NB):
            cbuf[b, 0:HIST, :] = cbuf[b, TB:TB + HIST, :]
        qk = _silu(conv)
        ml_q[...] = qk[:, 0:GROUP]
        ml_k[...] = qk[:, GROUP:2 * GROUP] * (HEAD_DIM ** -0.5)

    def prep_ml_v():
        ml_v[...] = proj(C_MLV, GROUP)
        zg = ml_gt[...] + gbias_ref[...]
        glane = lax.broadcasted_iota(jnp.int32, zg.shape, 1)
        ml_gt[...] = jnp.where(glane < N_HEADS, zg,
                               jnp.where(glane < 2 * N_HEADS, -_softplus(-zg), 0.0))

    rtp = rtp_ref[...]

    def rotary(z):
        ca, sa = cst_ref[:, 0:PAIR_W], cst_ref[:, PAIR_W:2 * PAIR_W]
        cos = ca * csr_ref[:, 0:PAIR_W] - sa * csr_ref[:, PAIR_W:2 * PAIR_W]
        sin = sa * csr_ref[:, 2 * PAIR_W:3 * PAIR_W] + ca * csr_ref[:, 3 * PAIR_W:4 * PAIR_W]
        cos, sin = per_seq(lambda b: cos), per_seq(lambda b: sin)
        even = lax.broadcasted_iota(jnp.int32, (T, GROUP), 1) % 2 == 0
        partner = jnp.where(even, pltpu.roll(z, GROUP - 1, axis=1), pltpu.roll(z, 1, axis=1))
        return z * jnp.concatenate([cos, cos], axis=1) + partner * jnp.concatenate([sin, sin], axis=1)

    def prep_rt(c0, dst, dst_scaled, scale, lo):
        def run():
            rot = rotary(proj(c0, GROUP)) * scale
            dst[...] = rot
            for c in range(n_chunks):
                rows = slice(c * CHUNK, (c + 1) * CHUNK)
                dst_scaled[rows, :] = rot[rows, :] * rtp[:, lo:lo + GROUP]
        return run

    def prep_rt_v():
        rt_v[...] = proj(C_RTV, GROUP)

    def gate_piece(c0, dst_ref, lo, act):
        def run():
            dst_ref[:, lo:lo + GROUP] = act(proj(c0, GROUP))
        return run

    def prep_pool_u():
        u = proj(C_PLU, GROUP)
        for b in range(NB):
            ubuf[b, POOL_HIST:POOL_HIST + TB, :] = u[seq_rows[b], :]

    def ple_piece(q):
        def run():
            cols = slice(q * PLE_COLS, (q + 1) * PLE_COLS)
            gate = _sigmoid(jnp.dot(xb, ple_gate_ref[:, cols], preferred_element_type=F32))
            emb = gate * _dot(per_seq(lambda b: p_ref[b]), ple_w_ref[:, cols])
            for b in range(NB):
                o_ref[b, :, cols] = emb[seq_rows[b], :]
        return run

    fillers = [prep_ml_qk, prep_ml_v,
               prep_rt(C_RTQ, rt_q, rt_qx, 1.0, 0), prep_rt(C_RTK, rt_k, rt_kz, HEAD_DIM ** -0.5, GROUP),
               prep_rt_v]
    n_late_fillers = 6
    fillers += [gate_piece(C_RWG, mixed, 0, _silu),
               prep_pool_u,
               gate_piece(C_PLG, mixed, GROUP, _silu),
               gate_piece(C_MLO, ml_og, 0, _sigmoid),
               gate_piece(C_MLG, mixed, 2 * GROUP, _silu),
               gate_piece(C_RTG, mixed, 3 * GROUP, _silu)]
    ple_pieces = [ple_piece(q) for q in range(D_MODEL // PLE_COLS)]

    def fill(n=1):
        for _ in range(min(n, len(fillers))):
            fillers.pop(0)()

    z1 = jnp.dot(xb, wr_ref[...], preferred_element_type=F32)
    zrw = z1[:, 0:RWKV_SHIFT]
    ml_gt[...] = z1[:, RWKV_SHIFT:RWKV_SHIFT + GATE_PAD]
    fill(2)
    for b in range(NB):
        zbuf[b, HIST:HIST + TB, :] = zrw[seq_rows[b], :]
    prev = per_seq(lambda b: zbuf[b, HIST - 1:HIST - 1 + TB, :])
    zs = zrw + (prev - zrw) * mu_ref[...]
    for b in range(NB):
        zbuf[b, 0:HIST, :] = zbuf[b, TB:TB + HIST, :]
    r = zs[:, 0:GROUP]
    k = zs[:, GROUP:2 * GROUP]
    v = zs[:, 2 * GROUP:3 * GROUP]
    wa = zs[:, 3 * GROUP:RWKV_SHIFT]
    lane = lax.broadcasted_iota(jnp.int32, wa.shape, 1)
    wa = jnp.where(lane < LORA, jnp.tanh(wa), wa)
    lo = _dot(wa, lora_ref[...])
    fill()
    rwp = rwp_ref[...]
    w0, a0, k_k, k_a, r_k, rw_ln_w, rw_ln_b, pl_scale = (rwp[i:i + 1, :] for i in range(8))
    w_log = -_softplus(-(w0 + lo[:, 0:GROUP])) - 0.5
    a = _sigmoid(a0 + lo[:, GROUP:2 * GROUP])
    kk = k * k_k
    kk = kk * lax.rsqrt(jnp.maximum(_segsum(kk * kk, seg), 1e-24))
    k2 = k * (1.0 + (a - 1.0) * k_a)
    rw_r[...] = r
    rw_k[...] = k2
    rw_v[...] = v
    rw_lw[...] = -jnp.exp(w_log)
    rw_kk[...] = kk
    rw_b[...] = kk * a
    bonus = _segsum(r * k2 * r_k, seg) * v

    PAIR = PAIR_W
    ri = lax.broadcasted_iota(jnp.int32, (PAIR, PAIR), 0)
    ci = lax.broadcasted_iota(jnp.int32, (PAIR, PAIR), 1)
    same_blk = (ri // HEAD_DIM) == (ci // HEAD_DIM)
    ti = lax.broadcasted_iota(jnp.int32, (CHUNK, PAIR), 0)
    si = lax.broadcasted_iota(jnp.int32, (CHUNK, PAIR), 1) % HEAD_DIM
    tri_s = ti > si
    tri_i = ti >= si
    eye = jnp.where(ti == si, 1.0, 0.0).astype(F32)

    def merge_mask(b):
        return ((ti // (2 * b)) == (si // (2 * b))) & ((ti // b) != (si // b))
    first = lax.broadcasted_iota(jnp.int32, (CHUNK, PAIR), 1) < HEAD_DIM
    first_row = lax.broadcasted_iota(jnp.int32, (1, 2 * PAIR), 1) % PAIR < HEAD_DIM
    e8 = jnp.where(lax.broadcasted_iota(jnp.int32, (SUBLANES, GATE_PAD), 0)
                   == lax.broadcasted_iota(jnp.int32, (SUBLANES, GATE_PAD), 1), 1.0, 0.0).astype(BF16)
    ones_bd = jnp.where(same_blk, 1.0, 0.0).astype(BF16)
    dec = dec_ref[...]
    g_chunk = rtp[0:1, 2 * GROUP:3 * GROUP]
    pairs = [(p, slice(p * PAIR, (p + 1) * PAIR)) for p in range(N_PAIRS)]

    def rows_of(c):
        if isinstance(c, int):
            return slice(c * CHUNK, (c + 1) * CHUNK)
        return pl.ds(pl.multiple_of(c * CHUNK, CHUNK), CHUNK)

    def dup(x):
        return jnp.concatenate([x, x], axis=0)

    def stack(x):
        return jnp.where(same_blk, dup(x), 0.0)

    def stack2(x):
        return jnp.concatenate([stack(x[:, 0:PAIR]), stack(x[:, PAIR:2 * PAIR])], axis=1)


    def state_scans():
        for b in range(NB):
            s_ml = [ml_state[b, p] for p, _ in pairs]
            m_ml = [ml_m[b, h:h + 1, 0:1] for h in range(N_HEADS)]
            s_rt = [rt_state[b, p] for p, _ in pairs]
            for c in range(b * chunks_per_seq, (b + 1) * chunks_per_seq):
                for p, ps in pairs:
                    s_old, s_new = [], []
                    for h in (2 * p, 2 * p + 1):
                        m_chunk = ml_sc[c, h:h + 1, 0:1]
                        b_last = ml_sc[c, N_HEADS + h:N_HEADS + h + 1, 0:1]
                        m_new = jnp.maximum(b_last + m_ml[h], m_chunk)
                        s_old.append(jnp.exp(b_last + m_ml[h] - m_new))
                        s_new.append(jnp.exp(m_chunk - m_new))
                        ml_sc[c, h:h + 1, :] = jnp.broadcast_to(m_ml[h], (1, LANES))
                        m_ml[h] = m_new
                    inc = ml_c[c, p]
                    ml_c[c, p] = s_ml[p]
                    s_ml[p] = (jnp.where(first_row, s_old[0], s_old[1]) * s_ml[p]
                               + jnp.where(first_row, s_new[0], s_new[1]) * inc)

                    inc = rt_r[c, p]
                    rt_r[c, p] = s_rt[p]
                    s_rt[p] = g_chunk[:, ps] * s_rt[p] + inc
            for p, _ in pairs:
                ml_state[b, p] = s_ml[p]
                rt_state[b, p] = s_rt[p]
            for h in range(N_HEADS):
                ml_m[b, h:h + 1, :] = jnp.broadcast_to(m_ml[h], (1, LANES))

    def pass1():
        cs = list(range(n_chunks))
        chains = [(j, p, ps) for j in range(n_chunks) for p, ps in pairs]

        prep = []
        for c in cs:
            rows = rows_of(c)
            lw = rw_lw[rows, :]
            cum = _dot_exact_lhs(ltri, lw)
            g_l = cum[CHUNK - 1:CHUNK, :]
            e_neg = jnp.exp(-cum)
            kk_c = rw_kk[rows, :]
            b_c = rw_b[rows, :]
            k_c = rw_k[rows, :]
            e_g = jnp.exp(g_l - cum)
            rw_dec[c] = jnp.broadcast_to(jnp.exp(g_l), (SUBLANES, GROUP))
            prep.append(dict(al=jnp.exp(cum - lw) * kk_c, be=b_c * e_neg, kt=k_c * e_neg,
                             rt=rw_r[rows, :] * jnp.exp(cum), kh=k_c * e_g, bh=b_c * e_g,
                             v=rw_v[rows, :]))
        pick = lambda name: [prep[j][name][:, ps] for j, p, ps in chains]
        al, be, kt, rt, kh, bh, vv = (pick(n) for n in ("al", "be", "kt", "rt", "kh", "bh", "v"))
        ar = [jnp.concatenate([a, r], axis=0) for a, r in zip(al, rt)]
        nrpq = [_nt(x, jnp.concatenate([stack(y), stack(z)], axis=0))
                for x, y, z in zip(ar, be, kt)]
        fill(2)
        n_m = [jnp.where(tri_s, t[0:CHUNK, 0:PAIR], 0.0) for t in nrpq]
        rb_m = [jnp.where(tri_i, t[CHUNK:2 * CHUNK, 0:PAIR], 0.0) for t in nrpq]
        p_m = [jnp.where(tri_s, t[0:CHUNK, PAIR:2 * PAIR], 0.0) for t in nrpq]
        q_m = [jnp.where(tri_i, t[CHUNK:2 * CHUNK, PAIR:2 * PAIR], 0.0) for t in nrpq]
        pqv = [_dot(jnp.concatenate([x, y], axis=0), stack(v)) for x, y, v in zip(p_m, q_m, vv)]
        fill()
        tinv = [eye - jnp.where(merge_mask(1), n, 0.0) for n in n_m]

        def merge_step(b, tinv):
            mask = merge_mask(b)
            dc = [_dot(t, stack(jnp.where(mask, n, 0.0))) for t, n in zip(tinv, n_m)]
            fill()
            out = [t - _dot(x, stack(t)) for t, x in zip(tinv, dc)]
            fill()
            return out

        tinv = merge_step(2, tinv)

        while len(fillers) > n_late_fillers:
            fill()
        late = []
        for c in cs:
            rows = rows_of(c)
            gt = ml_gt[rows, :]
            bcum = _dot_exact_lhs(ltri, gt)
            gl2 = lax.broadcasted_iota(jnp.int32, gt.shape, 1)
            gb = jnp.where(gl2 < N_HEADS, gt, bcum)
            gb_t = _nt_exact_lhs(e8, gb)
            ml_gb[rows, :] = gb
            ml_gbt[c] = gb_t
            b_last4 = gb_t[N_HEADS:2 * N_HEADS, CHUNK - 1:CHUNK]
            m_chunk4 = jnp.max(b_last4 - gb_t[N_HEADS:2 * N_HEADS, :] + gb_t[0:N_HEADS, :],
                               axis=-1, keepdims=True)
            ml_sc[c] = jnp.concatenate([jnp.broadcast_to(m_chunk4, (N_HEADS, LANES)),
                                        jnp.broadcast_to(b_last4, (N_HEADS, LANES))], axis=0)
            k_c2 = ml_k[rows, :]
            v_c2 = ml_v[rows, :]
            kz_c = rt_kz[rows, :]
            v_c3 = rt_v[rows, :]
            for p, ps in pairs:
                wg = []
                for h in (2 * p, 2 * p + 1):
                    b_col = gb[:, N_HEADS + h:N_HEADS + h + 1]
                    li_col = gb[:, h:h + 1]
                    wg.append(jnp.exp(b_last4[h:h + 1] - b_col + li_col - m_chunk4[h:h + 1]))
                wgt = jnp.where(first, wg[0], wg[1])
                late.append((c, p, k_c2[:, ps], jnp.concatenate([v_c2[:, ps] * wgt, wgt], axis=1),
                             kz_c[:, ps], v_c3[:, ps]))

        for b in (4, 8):
            tinv = merge_step(b, tinv)
        for c, p, k_p, vw, kz_p, v_p in late:
            cc = _tn(k_p, vw)
            ml_c[c, p] = jnp.concatenate([jnp.where(same_blk, cc[:, 0:PAIR], 0.0),
                                          jnp.where(same_blk, cc[:, PAIR:2 * PAIR], 0.0)], axis=1)
            rt_r[c, p] = jnp.where(same_blk, _tn(kz_p, v_p), 0.0)
        state_scans()
        for b in (16, 32):
            tinv = merge_step(b, tinv)
        wu = [_dot(t, jnp.concatenate([stack(a), stack(x[0:CHUNK])], axis=1))
              for t, a, x in zip(tinv, al, pqv)]
        fill()
        rbwu = [_dot(x, stack2(y)) for x, y in zip(rb_m, wu)]
        gu = [_tn(y, x) for y, x in zip(wu, bh)]
        fill()
        vk = [_tn(x, y) for x, y in zip(vv, kh)]
        for n, (j, p, ps) in enumerate(chains):
            rw_g[cs[j], p] = jnp.where(same_blk, gu[n][0:PAIR], 0.0)
            rw_h[cs[j], p] = jnp.where(same_blk, vk[n] - gu[n][PAIR:2 * PAIR], 0.0)
            rows = rows_of(cs[j])
            rw_rw[rows, ps] = rt[n] - rbwu[n][:, 0:PAIR]
            rw_y[rows, ps] = pqv[n][CHUNK:2 * CHUNK] - rbwu[n][:, PAIR:2 * PAIR]

    pass1()
    while fillers:
        fill()

    neg_inf = jnp.full((CHUNK, PAIR), -jnp.inf, F32)

    def scores(c):
        rows = rows_of(c)
        ld = lambda ref: [ref[rows, ps] for p, ps in pairs]
        q_c, k_c2, q_c3, qx_c, k_c3 = (ld(ref) for ref in (ml_q, ml_k, rt_q, rt_qx, rt_k))
        return dict(
            qk_ml=[_nt(q_c[p], stack(k_c2[p])) for p, _ in pairs],
            qk_rt=[_nt(q_c3[p], stack(k_c3[p])) for p, _ in pairs],
            in_ml=[_dot(q_c[p], ml_c[c, p]) for p, _ in pairs],
            in_rt=[_dot(qx_c[p], rt_r[c, p]) for p, _ in pairs])

    s_rw_all = [[rw_state[b, p] for p, _ in pairs] for b in range(NB)]
    order = [(b, b * chunks_per_seq + j) for j in range(chunks_per_seq) for b in range(NB)]
    nxt = scores(order[0][1])
    for n_done, (b, c) in enumerate(order):
        cur, nxt = nxt, (scores(order[n_done + 1][1]) if n_done + 1 < len(order) else None)
        qk_ml, qk_rt, in_ml, in_rt = cur["qk_ml"], cur["qk_rt"], cur["in_ml"], cur["in_rt"]
        s_rw = s_rw_all[b]
        rows = rows_of(c)
        v_c2 = [ml_v[rows, ps] for p, ps in pairs]
        v_c3 = [rt_v[rows, ps] for p, ps in pairs]
        y_rw = [_nt(rw_rw[rows, ps], s_rw[p]) for p, ps in pairs]
        sg = [_dot(s_rw[p], rw_g[c, p]) for p, _ in pairs]

        gb = ml_gb[rows, :]
        gb_t = ml_gbt[c]
        sc_ml, floor, s_int = [], [], []
        for p, ps in pairs:
            h0, h1 = 2 * p, 2 * p + 1
            b_col = jnp.where(first, gb[:, N_HEADS + h0:N_HEADS + h0 + 1],
                              gb[:, N_HEADS + h1:N_HEADS + h1 + 1])
            b_row = jnp.concatenate([gb_t[N_HEADS + h0:N_HEADS + h0 + 1, :],
                                     gb_t[N_HEADS + h1:N_HEADS + h1 + 1, :]], axis=1)
            li_row = jnp.concatenate([gb_t[h0:h0 + 1, :], gb_t[h1:h1 + 1, :]], axis=1)
            d_log = jnp.where(tri_i, b_col - b_row + li_row, neg_inf)
            m_intra = jnp.where(first,
                                jnp.max(jnp.where(first, d_log, neg_inf), axis=-1, keepdims=True),
                                jnp.max(jnp.where(first, neg_inf, d_log), axis=-1, keepdims=True))
            m_inter = b_col + jnp.where(first, ml_sc[c, h0:h0 + 1, 0:1], ml_sc[c, h1:h1 + 1, 0:1])
            m_t = jnp.maximum(m_inter, m_intra)
            floor.append(jnp.exp(-m_t))
            s_int.append(jnp.exp(m_inter - m_t))
            sc_ml.append(qk_ml[p] * jnp.exp(d_log - m_t))
        sc_rt = [qk_rt[p] * dec[:, ps] for p, ps in pairs]
        tot_ml = [_dot(sc_ml[p], jnp.concatenate([stack(v_c2[p]).astype(BF16), ones_bd], axis=1))
                  for p, _ in pairs]
        intra_rt = [_dot(sc_rt[p], stack(v_c3[p])) for p, _ in pairs]
        for piece in ple_pieces[n_done * len(ple_pieces) // n_chunks:(n_done + 1) * len(ple_pieces) // n_chunks]:
            piece()
        for p, ps in pairs:
            tot = tot_ml[p] + jnp.concatenate([s_int[p], s_int[p]], axis=1) * in_ml[p]
            rw_y[rows, ps] = rw_y[rows, ps] + y_rw[p]
            ml_y[rows, ps] = tot[:, 0:PAIR] / jnp.maximum(jnp.abs(tot[:, PAIR:2 * PAIR]), floor[p])
            rt_y[rows, ps] = intra_rt[p] + in_rt[p]
            s_rw[p] = s_rw[p] * rw_dec[c, 0:1, ps] - sg[p] + rw_h[c, p]
    for b in range(NB):
        for p, _ in pairs:
            rw_state[b, p] = s_rw_all[b][p]

    y_a = (_head_norm(rw_y[...], seg_mean, RWKV_GN_EPS) * rw_ln_w + rw_ln_b + bonus)
    mixed[:, 0:GROUP] = y_a * mixed[:, 0:GROUP]

    pgrp = lax.broadcasted_iota(jnp.int32, (TB, GROUP), 1) // (GROUP // len(POOL_WINDOWS))

    def window_sums(b):
        sums = [ubuf[b]]
        for sh in (1, 2, 4, 8):
            sums.append(sums[-1] + pltpu.roll(sums[-1], sh, axis=0))
        wsum = sums[1][POOL_HIST:, :]
        for gi in range(1, len(POOL_WINDOWS)):
            wsum = jnp.where(pgrp == gi, sums[gi + 1][POOL_HIST:, :], wsum)
        return wsum

    wsum = per_seq(window_sums)
    u_all = per_seq(lambda b: ubuf[b, POOL_HIST:POOL_HIST + TB, :])
    pos = per_seq(lambda b: (t_idx * TB + lax.broadcasted_iota(jnp.int32, (TB, 1), 0) + 1).astype(F32))
    win = jnp.exp2((pgrp[0:1, :] + 1).astype(F32))
    mean = wsum * jnp.where(pos >= win, 1.0 / win, 1.0 / pos)
    for b in range(NB):
        ubuf[b, 0:POOL_HIST, :] = ubuf[b, TB:TB + POOL_HIST, :]
    y_b = _dot(mean - u_all, plw_ref[...]) * pl_scale
    mixed[:, GROUP:2 * GROUP] = y_b * mixed[:, GROUP:2 * GROUP]

    hcm = ml_y[...] * ml_og[...]
    y_c = _head_norm(hcm, seg_mean, HEAD_NORM_EPS) * mlln_ref[...]
    mixed[:, 2 * GROUP:3 * GROUP] = y_c * mixed[:, 2 * GROUP:3 * GROUP]

    y_d = _head_norm(rt_y[...], seg_mean, HEAD_NORM_EPS)
    mixed[:, 3 * GROUP:4 * GROUP] = y_d * mixed[:, 3 * GROUP:4 * GROUP]

    lnp = lnp_ref[...]
    for b in range(NB):
        for r0 in range(0, TB, OUT_ROWS):
            rs = slice(r0, min(r0 + OUT_ROWS, TB))
            ms = slice(b * TB + rs.start, b * TB + rs.stop)
            hres = alpha * x_ref[b, rs, :] + _dot(mixed[ms, :], w_out_ref[...]) + o_ref[b, rs, :]
            mu_ln = jnp.mean(hres, axis=-1, keepdims=True)
            hc = hres - mu_ln
            var = jnp.mean(hc * hc, axis=-1, keepdims=True)
            o_ref[b, rs, :] = hc * lax.rsqrt(var + LN_EPS) * lnp[0:1, :] + lnp[1:2, :]


def _full(shape):
    return pl.BlockSpec(shape, lambda t: (0,) * len(shape))


def _per_layer(arr, layer):
    nd = arr.ndim - 1
    return pl.BlockSpec((None,) + arr.shape[1:], lambda t: (layer,) + (0,) * nd)


def _layer_call(x, p, layer_params, shared, *, layer, alpha, tile_t):
    B, S, _ = x.shape
    TB = tile_t
    T = B * TB
    grid = (S // TB,)
    tile = lambda width: pl.BlockSpec((B, TB, width), lambda t: (0, t, 0))
    in_specs = [tile(D_MODEL),
                pl.BlockSpec((None, B, TB, D_PLE), lambda t: (layer, 0, t, 0))]
    names = ["wr", "wa", "wb", "w_out", "ple_gate", "ple_w", "mu", "rwp", "lora", "plw", "conv",
             "gbias", "mlln", "lnp"]
    in_specs += [_per_layer(layer_params[n], layer) for n in names]
    in_specs += [pl.BlockSpec((None, 1, 2 * PAIR_W), lambda t: (t, 0, 0)),
                 _full(shared["csr"].shape)]
    tail = ["dec", "rtp", "seg", "ltri"]
    in_specs += [_full(shared[n].shape) for n in tail]
    act = lambda width: pltpu.VMEM((T, width), F32)
    n_chunks = T // CHUNK
    per_chunk = lambda r, w: pltpu.VMEM((n_chunks, N_PAIRS, r, w), F32)
    scratch = [
        pltpu.VMEM((B, TB + HIST, RWKV_SHIFT), F32),
        pltpu.VMEM((B, TB + HIST, 2 * GROUP), F32),
        pltpu.VMEM((B, TB + POOL_HIST, GROUP), F32),
        act(GROUP), act(GROUP), act(GROUP), act(GROUP), act(GROUP), act(GROUP), act(GROUP),
        pltpu.VMEM((B, N_PAIRS, PAIR_W, PAIR_W), F32),
        act(GROUP),
        per_chunk(PAIR_W, PAIR_W), per_chunk(PAIR_W, PAIR_W),
        pltpu.VMEM((n_chunks, SUBLANES, GROUP), F32),
        act(GROUP), act(GROUP), act(GROUP), act(GATE_PAD), act(GROUP),
        pltpu.VMEM((B, N_PAIRS, PAIR_W, 2 * PAIR_W), F32),
        pltpu.VMEM((B, SUBLANES, LANES), F32),
        per_chunk(PAIR_W, 2 * PAIR_W),
        act(GATE_PAD),
        pltpu.VMEM((n_chunks, SUBLANES, CHUNK), F32),
        pltpu.VMEM((n_chunks, SUBLANES, LANES), F32),
        act(GROUP),
        act(GROUP), act(GROUP), act(GROUP), act(GROUP), act(GROUP), act(GROUP),
        pltpu.VMEM((B, N_PAIRS, PAIR_W, PAIR_W), F32),
        per_chunk(PAIR_W, PAIR_W),
        act(D_MODEL),
    ]
    args = ([x, p] + [layer_params[n] for n in names] + [shared["cst"], shared["csr"]]
            + [shared[n] for n in tail])
    return pl.pallas_call(
        functools.partial(_layer_kernel, alpha=alpha),
        grid=grid,
        in_specs=in_specs,
        out_specs=tile(D_MODEL),
        out_shape=jax.ShapeDtypeStruct((B, S, D_MODEL), F32),
        scratch_shapes=scratch,
        compiler_params=pltpu.CompilerParams(
            dimension_semantics=("arbitrary",),
            vmem_limit_bytes=VMEM_LIMIT_BYTES),
        name="deepnorm_mixer_layer",
    )(*args)


def _block_diag(blocks):
    n = len(blocks)
    rows = []
    for i, blk in enumerate(blocks):
        z = jnp.zeros_like(blk)
        rows.append(jnp.concatenate([blk if j == i else z for j in range(n)], axis=-1))
    return jnp.concatenate(rows, axis=-2)


def kernel(x, p, w_in, rw_mu, rw_w0, rw_w2, rw_a0, rw_a2, rw_kk, rw_ka, rw_rk, rw_ln_w, rw_ln_b, pl_w, pl_scale, ml_conv, ml_ib, ml_fb, ml_ln_w, w_out, ple_w, ple_gate, ln_g, ln_b):
    B, S, D = x.shape
    depth = w_in.shape[0]
    alpha = (2.0 * depth) ** 0.25
    tile_t = min(TILE_T, S)
    assert D == D_MODEL and S % tile_t == 0 and tile_t % CHUNK == 0

    inv_freq = ROPE_BASE ** (-jnp.arange(0, HEAD_DIM, 2, dtype=F32) / HEAD_DIM)
    freq = jnp.tile(jnp.repeat(inv_freq, 2), 2)
    sgn = jnp.tile(jnp.asarray([-1.0, 1.0], F32), HEAD_DIM)
    ang_t = (jnp.arange(S // tile_t, dtype=F32) * tile_t)[:, None] * freq
    ang_r = jnp.arange(tile_t, dtype=F32)[:, None] * freq
    cst = jnp.concatenate([jnp.cos(ang_t), jnp.sin(ang_t)], axis=1)[:, None, :]
    csr = jnp.concatenate([jnp.cos(ang_r), jnp.sin(ang_r), jnp.cos(ang_r) * sgn, jnp.sin(ang_r) * sgn], axis=1)
    log_g = jnp.log1p(-jnp.exp2(-5.0 - jnp.arange(N_HEADS, dtype=F32)))
    cpos = jnp.arange(CHUNK, dtype=F32)
    rel = cpos[:, None] - cpos[None, :]
    dec = jnp.where(rel >= 0, jnp.exp(log_g[:, None, None] * jnp.maximum(rel, 0.0)), 0.0)
    dec = dec.transpose(1, 0, 2).reshape(CHUNK, GROUP)
    zeta = jnp.exp(log_g[:, None] * (CHUNK - 1.0 - cpos))
    xi = jnp.exp(log_g[:, None] * (cpos + 1.0))
    g_chunk = jnp.exp(log_g * CHUNK)
    lanes = lambda t: jnp.repeat(t.T, HEAD_DIM, axis=1)
    rtp = jnp.concatenate([lanes(xi), lanes(zeta),
                           jnp.broadcast_to(jnp.repeat(g_chunk, HEAD_DIM)[None, :], (CHUNK, GROUP))], axis=1)
    head_of = np.arange(GROUP) // HEAD_DIM
    shared = {
        "cst": cst, "csr": csr, "dec": dec, "rtp": rtp,
        "seg": jnp.asarray(head_of[:, None] == head_of[None, :], BF16),
        "ltri": jnp.asarray(np.tril(np.ones((CHUNK, CHUNK))), BF16),
    }

    g0 = C_MLO
    g1 = g0 + 2 * N_HEADS
    w_in_b = w_in.astype(BF16)
    layer_params = {
        "wr": jnp.concatenate([w_in_b[:, :, :C_RWG],
                               jnp.pad(w_in_b[:, :, g0:g1], ((0, 0), (0, 0), (0, GATE_PAD - 2 * N_HEADS)))], axis=2),
        "wa": w_in_b[:, :, C_RWG:g0],
        "wb": w_in_b[:, :, g1:],
        "w_out": w_out.astype(BF16),
        "ple_gate": ple_gate.astype(BF16),
        "ple_w": ple_w.astype(BF16),
        "mu": rw_mu[:, None, :].astype(F32),
        "rwp": jnp.stack([rw_w0, rw_a0, rw_kk, rw_ka, rw_rk, rw_ln_w, rw_ln_b, pl_scale], axis=1).astype(F32),
        "lora": _block_diag([rw_w2, rw_a2]).astype(BF16),
        "plw": _block_diag([pl_w[:, g] for g in range(len(POOL_WINDOWS))]).astype(BF16),
        "conv": jnp.pad(ml_conv.astype(F32), ((0, 0), (0, 8 - CONV_K), (0, 0))),
        "gbias": jnp.pad(jnp.concatenate([ml_ib, ml_fb], axis=1).astype(F32),
                         ((0, 0), (0, GATE_PAD - 2 * N_HEADS)))[:, None, :],
        "mlln": ml_ln_w[:, None, :].astype(F32),
        "lnp": jnp.stack([ln_g, ln_b], axis=1).astype(F32),
    }
    for i in range(depth):
        x = _layer_call(x, p, layer_params, shared, layer=i, alpha=alpha, tile_t=tile_t)
    return x
```
